```python
import jax
import jax.numpy as jnp
from jax import lax
import numpy as np

D_MODEL = 1024
BATCH = 8
SEQ = 2048
DEPTH = 4

N_MIXERS = 3
NORM_EPS = 1e-6
ROPE_THETA = 10000.0

RET_HEADS = 4
RET_QK_DIM = D_MODEL // RET_HEADS
RET_V_DIM = 2 * D_MODEL // RET_HEADS
RET_CHUNK = 128

LRU_WIDTH = -(-(4 * D_MODEL // 3) // 256) * 256
LRU_BLOCKS = 8
LRU_BLOCK_W = LRU_WIDTH // LRU_BLOCKS
LRU_C = 8.0
CONV_WIDTH = 4

SWA_HEAD_DIM = 64
SWA_Q_HEADS = D_MODEL // SWA_HEAD_DIM
SWA_KV_HEADS = 4
SWA_GROUP = SWA_Q_HEADS // SWA_KV_HEADS
SWA_WINDOW = 128
SWA_BLOCK = 128

FFN_HIDDEN = -(-(8 * D_MODEL // 3) // 256) * 256

N_RET = len(range(0, DEPTH, N_MIXERS))
N_LRU = len(range(1, DEPTH, N_MIXERS))
N_SWA = len(range(2, DEPTH, N_MIXERS))

kernel_name = 'hybrid_retention_rglru_swa_sink_decoder'


def rms_norm(x, gain):
    xf = x.astype(jnp.float32)
    y = xf * lax.rsqrt(jnp.mean(xf * xf, axis=-1, keepdims=True) + NORM_EPS)
    return (y * gain.astype(jnp.float32)).astype(x.dtype)


def apply_rope(t, positions):
    dh = t.shape[-1]
    half = dh // 2
    inv_freq = ROPE_THETA ** (-jnp.arange(half, dtype=jnp.float32) * 2.0 / dh)
    ang = positions.astype(jnp.float32)[..., None] * inv_freq
    cos = jnp.cos(ang)[:, :, None, :]
    sin = jnp.sin(ang)[:, :, None, :]
    tf = t.astype(jnp.float32)
    t1, t2 = tf[..., :half], tf[..., half:]
    return jnp.concatenate([t1 * cos - t2 * sin, t2 * cos + t1 * sin], axis=-1).astype(t.dtype)


def retention(h, positions, w_in, w_out):
    B, S, _ = h.shape
    H, dk, dv, C = RET_HEADS, RET_QK_DIM, RET_V_DIM, RET_CHUNK
    nc = S // C
    q, k, v, g = jnp.split(h @ w_in, [H * dk, 2 * H * dk, 2 * H * dk + H * dv], axis=-1)
    q = apply_rope(q.reshape(B, S, H, dk), positions)
    k = apply_rope(k.reshape(B, S, H, dk), positions) * (dk ** -0.5)
    v = v.reshape(B, S, H, dv)

    def to_chunks(t):
        return t.reshape(B, nc, C, H, t.shape[-1]).transpose(1, 0, 3, 2, 4)

    log_gamma = jnp.log1p(-(2.0 ** (-5.0 - jnp.arange(H, dtype=jnp.float32))))
    idx = jnp.arange(C, dtype=jnp.float32)
    rel = idx[:, None] - idx[None, :]
    intra = jnp.where(rel >= 0, jnp.exp(jnp.maximum(rel, 0.0) * log_gamma[:, None, None]), 0.0).astype(h.dtype)
    q_decay = jnp.exp((idx + 1.0)[None, :] * log_gamma[:, None]).astype(h.dtype)
    k_decay = jnp.exp((C - 1.0 - idx)[None, :] * log_gamma[:, None]).astype(h.dtype)
    chunk_decay = jnp.exp(C * log_gamma).astype(h.dtype)

    def step(state, inp):
        qi, ki, vi = inp
        scores = jnp.einsum('bhqd,bhkd->bhqk', qi, ki) * intra[None]
        o_inner = jnp.einsum('bhqk,bhkv->bhqv', scores, vi)
        o_cross = jnp.einsum('bhqd,bhdv->bhqv', qi, state) * q_decay[None, :, :, None]
        new_state = state * chunk_decay[None, :, None, None] + jnp.einsum(
            'bhkd,bhkv->bhdv', ki * k_decay[None, :, :, None], vi)
        return new_state, o_inner + o_cross

    state0 = jnp.zeros((B, H, dk, dv), h.dtype)
    _, o = lax.scan(step, state0, (to_chunks(q), to_chunks(k), to_chunks(v)))
    o = o.transpose(1, 0, 3, 2, 4).reshape(B, S, H, dv)
    of = o.astype(jnp.float32)
    o = (of * lax.rsqrt(jnp.mean(of * of, axis=-1, keepdims=True) + NORM_EPS)).astype(h.dtype)
    o = o.reshape(B, S, H * dv) * jax.nn.silu(g)
    return o @ w_out


def rglru_block(h, w_in, conv_w, conv_b, w_r, b_r, w_i, b_i, lam, w_out):
    B, S, _ = h.shape
    gate, u = jnp.split(h @ w_in, 2, axis=-1)
    u = lax.conv_general_dilated(
        u, conv_w[:, None, :].astype(u.dtype), window_strides=(1,),
        padding=[(CONV_WIDTH - 1, 0)], dimension_numbers=('NWC', 'WIO', 'NWC'),
        feature_group_count=LRU_WIDTH) + conv_b
    ub = u.reshape(B, S, LRU_BLOCKS, LRU_BLOCK_W)
    r = jax.nn.sigmoid(jnp.einsum('bsnc,ncd->bsnd', ub, w_r) + b_r).reshape(B, S, LRU_WIDTH)
    i = jax.nn.sigmoid(jnp.einsum('bsnc,ncd->bsnd', ub, w_i) + b_i).reshape(B, S, LRU_WIDTH)
    log_a = (-LRU_C * jax.nn.softplus(-lam.astype(jnp.float32))) * r.astype(jnp.float32)
    a = jnp.exp(log_a)
    b = jnp.sqrt(-jnp.expm1(2.0 * log_a)) * (i * u).astype(jnp.float32)

    def combine(left, right):
        a_l, b_l = left
        a_rt, b_rt = right
        return a_l * a_rt, a_rt * b_l + b_rt

    _, hs = lax.associative_scan(combine, (a, b), axis=1)
    y = hs.astype(h.dtype) * jax.nn.gelu(gate)
    return y @ w_out


def sliding_window_sink_attention(h, positions, w_qkv, b_qkv, sinks, w_out, b_out):
    B, S, _ = h.shape
    Hq, Hkv, G, dh, L = SWA_Q_HEADS, SWA_KV_HEADS, SWA_GROUP, SWA_HEAD_DIM, SWA_BLOCK
    nb = S // L
    q, k, v = jnp.split(h @ w_qkv + b_qkv, [Hq * dh, (Hq + Hkv) * dh], axis=-1)
    q = apply_rope(q.reshape(B, S, Hq, dh), positions) * (dh ** -0.5)
    k = apply_rope(k.reshape(B, S, Hkv, dh), positions)
    v = v.reshape(B, S, Hkv, dh)
    qb = q.reshape(B, nb, L, Hkv, G, dh)
    kb = k.reshape(B, nb, L, Hkv, dh)
    vb = v.reshape(B, nb, L, Hkv, dh)

    def with_prev(t):
        prev = jnp.concatenate([jnp.zeros_like(t[:, :1]), t[:, :-1]], axis=1)
        return jnp.concatenate([prev, t], axis=2)

    kw, vw = with_prev(kb), with_prev(vb)
    scores = jnp.einsum('bnqkgd,bnskd->bnkgqs', qb, kw).astype(jnp.float32)
    qi = jnp.arange(L)[:, None]
    si = jnp.arange(2 * L)[None, :]
    dist = qi + L - si
    in_window = (dist >= 0) & (dist < SWA_WINDOW)
    has_key = (jnp.arange(nb)[:, None, None] > 0) | (si[None] >= L)
    mask = in_window[None] & has_key
    scores = jnp.where(mask[None, :, None, None], scores, -jnp.inf)
    sink = jnp.broadcast_to(sinks.astype(jnp.float32).reshape(1, 1, Hkv, G, 1, 1), scores.shape[:-1] + (1,))
    probs = jax.nn.softmax(jnp.concatenate([scores, sink], axis=-1), axis=-1)[..., :-1].astype(h.dtype)
    o = jnp.einsum('bnkgqs,bnskd->bnqkgd', probs, vw).reshape(B, S, Hq * dh)
    return o @ w_out + b_out


def swiglu_ffn(h, w_gu, w_down):
    gate, up = jnp.split(h @ w_gu, 2, axis=-1)
    return (jax.nn.silu(gate) * up) @ w_down


def setup_inputs(seed: int = 0) -> dict:
    key = jax.random.key(seed)
    ks = jax.random.split(key, 24)
    f32 = jnp.float32

    def dense(k, shape, fan_in):
        return jax.random.normal(k, shape, f32) * (fan_in ** -0.5)

    def small(k, shape, scale):
        return scale * jax.random.normal(k, shape, f32)

    x = jax.random.normal(ks[0], (BATCH, SEQ, D_MODEL), f32)
    start = jax.random.randint(ks[1], (BATCH, 1), 0, SEQ, dtype=jnp.int32)
    positions = start + jnp.arange(SEQ, dtype=jnp.int32)[None, :]
    mix_norm = 1.0 + small(ks[2], (DEPTH, D_MODEL), 0.02)
    ffn_norm = 1.0 + small(ks[3], (DEPTH, D_MODEL), 0.02)
    final_norm = 1.0 + small(ks[4], (D_MODEL,), 0.02)

    ret_in_dim = 2 * RET_HEADS * RET_QK_DIM + 2 * RET_HEADS * RET_V_DIM
    ret_w_in = dense(ks[5], (N_RET, D_MODEL, ret_in_dim), D_MODEL)
    ret_w_out = dense(ks[6], (N_RET, RET_HEADS * RET_V_DIM, D_MODEL), RET_HEADS * RET_V_DIM)

    lru_w_in = dense(ks[7], (N_LRU, D_MODEL, 2 * LRU_WIDTH), D_MODEL)
    lru_conv_w = dense(ks[8], (N_LRU, CONV_WIDTH, LRU_WIDTH), CONV_WIDTH)
    lru_conv_b = small(ks[9], (N_LRU, LRU_WIDTH), 0.01)
    lru_w_r = dense(ks[10], (N_LRU, LRU_BLOCKS, LRU_BLOCK_W, LRU_BLOCK_W), LRU_BLOCK_W)
    lru_b_r = small(ks[11], (N_LRU, LRU_BLOCKS, LRU_BLOCK_W), 0.1)
    lru_w_i = dense(ks[12], (N_LRU, LRU_BLOCKS, LRU_BLOCK_W, LRU_BLOCK_W), LRU_BLOCK_W)
    lru_b_i = small(ks[13], (N_LRU, LRU_BLOCKS, LRU_BLOCK_W), 0.1)
    a_pow_c = jax.random.uniform(ks[14], (N_LRU, LRU_WIDTH), f32, 0.9, 0.999)
    a_base = a_pow_c ** (1.0 / LRU_C)
    lru_lambda = jnp.log(a_base) - jnp.log1p(-a_base)
    lru_w_out = dense(ks[15], (N_LRU, LRU_WIDTH, D_MODEL), LRU_WIDTH)

    qkv_dim = (SWA_Q_HEADS + 2 * SWA_KV_HEADS) * SWA_HEAD_DIM
    swa_w_qkv = dense(ks[16], (N_SWA, D_MODEL, qkv_dim), D_MODEL)
    swa_b_qkv = small(ks[17], (N_SWA, qkv_dim), 0.01)
    swa_sinks = small(ks[18], (N_SWA, SWA_Q_HEADS), 0.5)
    swa_w_out = dense(ks[19], (N_SWA, SWA_Q_HEADS * SWA_HEAD_DIM, D_MODEL), SWA_Q_HEADS * SWA_HEAD_DIM)
    swa_b_out = small(ks[20], (N_SWA, D_MODEL), 0.01)

    ffn_w_gu = dense(ks[21], (DEPTH, D_MODEL, 2 * FFN_HIDDEN), D_MODEL)
    ffn_w_down = dense(ks[22], (DEPTH, FFN_HIDDEN, D_MODEL), FFN_HIDDEN)

    return {'x': x, 'positions': positions, 'mix_norm': mix_norm, 'ffn_norm': ffn_norm,
            'final_norm': final_norm, 'ret_w_in': ret_w_in, 'ret_w_out': ret_w_out,
            'lru_w_in': lru_w_in, 'lru_conv_w': lru_conv_w, 'lru_conv_b': lru_conv_b,
            'lru_w_r': lru_w_r, 'lru_b_r': lru_b_r, 'lru_w_i': lru_w_i, 'lru_b_i': lru_b_i,
            'lru_lambda': lru_lambda, 'lru_w_out': lru_w_out, 'swa_w_qkv': swa_w_qkv,
            'swa_b_qkv': swa_b_qkv, 'swa_sinks': swa_sinks, 'swa_w_out': swa_w_out,
            'swa_b_out': swa_b_out, 'ffn_w_gu': ffn_w_gu, 'ffn_w_down': ffn_w_down}


def reference(x, positions, mix_norm, ffn_norm, final_norm, ret_w_in, ret_w_out,
              lru_w_in, lru_conv_w, lru_conv_b, lru_w_r, lru_b_r, lru_w_i, lru_b_i,
              lru_lambda, lru_w_out, swa_w_qkv, swa_b_qkv, swa_sinks, swa_w_out,
              swa_b_out, ffn_w_gu, ffn_w_down):
    for layer in range(DEPTH):
        kind = layer % N_MIXERS
        j = layer // N_MIXERS
        h = rms_norm(x, mix_norm[layer])
        if kind == 0:
            y = retention(h, positions, ret_w_in[j], ret_w_out[j])
        elif kind == 1:
            y = rglru_block(h, lru_w_in[j], lru_conv_w[j], lru_conv_b[j], lru_w_r[j], lru_b_r[j],
                            lru_w_i[j], lru_b_i[j], lru_lambda[j], lru_w_out[j])
        else:
            y = sliding_window_sink_attention(h, positions, swa_w_qkv[j], swa_b_qkv[j],
                                              swa_sinks[j], swa_w_out[j], swa_b_out[j])
        x = x + y
        x = x + swiglu_ffn(rms_norm(x, ffn_norm[layer]), ffn_w_gu[layer], ffn_w_down[layer])
    return rms_norm(x, final_norm)
```

```python
import functools
import math

import jax
import jax.numpy as jnp
from jax import lax
from jax.experimental import pallas as pl
from jax.experimental.pallas import tpu as pltpu

F32 = jnp.float32
BF16 = jnp.bfloat16

D_MODEL = 1024
DEPTH = 4
N_MIXERS = 3
NORM_EPS = 1e-6
ROPE_THETA = 10000.0

RET_HEADS = 4
RET_QK = 256
RET_V = 512
RET_CHUNK = 128

LRU_W = 1536
LRU_BLOCKS = 8
LRU_BLOCK_W = 192
LRU_PAIR_W = 2 * LRU_BLOCK_W
LRU_PAIRS = LRU_BLOCKS // 2
LRU_C = 8.0
CONV_W = 4
CONV_HALO = 8

SWA_DH = 64
SWA_Q_HEADS = 16
SWA_KV_HEADS = 4
SWA_WINDOW = 128
SWA_BLOCK = 128
LANES = 128

FFN_H = 2816
FFN_CHUNK = 256

VMEM_LIMIT = 56 * 1024 * 1024

TM_FFN = 512
TM_RET = 512
TM_LRU = 256
TM_SWA = 512


def _rms(x, gain):
    return x * lax.rsqrt(jnp.mean(x * x, axis=-1, keepdims=True) + NORM_EPS) * gain


def _const_spec(shape):
    zeros = (0,) * len(shape)
    return pl.BlockSpec(shape, lambda *_: zeros, pipeline_mode=pl.Buffered(1))


def _params(n_axes):
    return pltpu.CompilerParams(
        dimension_semantics=("arbitrary",) * n_axes, vmem_limit_bytes=VMEM_LIMIT)


def _ffn_kernel(x_ref, gain_ref, wgu_ref, wd_ref, fin_ref, o_ref, act_ref, *, final):
    x = x_ref[...]
    h = _rms(x, gain_ref[...]).astype(BF16)
    for c in range(FFN_H // FFN_CHUNK):
        lo = c * FFN_CHUNK
        g = jnp.dot(h, wgu_ref[:, lo:lo + FFN_CHUNK], preferred_element_type=F32)
        u = jnp.dot(h, wgu_ref[:, FFN_H + lo:FFN_H + lo + FFN_CHUNK], preferred_element_type=F32)
        act_ref[:, lo:lo + FFN_CHUNK] = (g * jax.nn.sigmoid(g) * u).astype(BF16)
    y = x + jnp.dot(act_ref[...], wd_ref[...], preferred_element_type=F32)
    if final:
        y = _rms(y, fin_ref[...])
    o_ref[...] = y


def _ffn(x2d, gain, wgu, wd, fin, final):
    t = x2d.shape[0]
    tm = TM_FFN
    return pl.pallas_call(
        functools.partial(_ffn_kernel, final=final),
        grid=(t // tm,),
        in_specs=[
            pl.BlockSpec((tm, D_MODEL), lambda i: (i, 0)),
            _const_spec((1, D_MODEL)),
            _const_spec(wgu.shape),
            _const_spec(wd.shape),
            _const_spec((1, D_MODEL)),
        ],
        out_specs=pl.BlockSpec((tm, D_MODEL), lambda i: (i, 0)),
        out_shape=jax.ShapeDtypeStruct(x2d.shape, F32),
        scratch_shapes=[pltpu.VMEM((tm, FFN_H), BF16)],
        compiler_params=_params(1),
        name="ffn",
    )(x2d, gain, wgu, wd, fin)


def _ret_kernel(x_ref, pos_ref, gain_ref, win_ref, wout_ref, invf_ref, intra_ref,
                qdec_ref, kdec_ref, o_ref,
                q_s, k_s, kd_s, v_s, g_s, og_s, state_s, *, chunk_decay):
    tm = x_ref.shape[0]
    hq = RET_HEADS * RET_QK
    hv = RET_HEADS * RET_V
    half = RET_QK // 2

    @pl.when(pl.program_id(1) == 0)
    def _():
        state_s[...] = jnp.zeros_like(state_s)

    x = x_ref[...]
    h = _rms(x, gain_ref[...]).astype(BF16)
    ang = pos_ref[...].astype(F32) * invf_ref[...]
    cos = jnp.cos(ang)
    sin = jnp.sin(ang)

    q = jnp.dot(h, win_ref[:, 0:hq], preferred_element_type=F32)
    for hd in range(RET_HEADS):
        t1 = q[:, hd * RET_QK:hd * RET_QK + half]
        t2 = q[:, hd * RET_QK + half:(hd + 1) * RET_QK]
        q_s[:, hd * RET_QK:hd * RET_QK + half] = (t1 * cos - t2 * sin).astype(BF16)
        q_s[:, hd * RET_QK + half:(hd + 1) * RET_QK] = (t2 * cos + t1 * sin).astype(BF16)
    k = jnp.dot(h, win_ref[:, hq:2 * hq], preferred_element_type=F32)
    scale = RET_QK ** -0.5
    for hd in range(RET_HEADS):
        t1 = k[:, hd * RET_QK:hd * RET_QK + half]
        t2 = k[:, hd * RET_QK + half:(hd + 1) * RET_QK]
        r1 = (t1 * cos - t2 * sin) * scale
        r2 = (t2 * cos + t1 * sin) * scale
        kdec = kdec_ref[hd]
        k_s[:, hd * RET_QK:hd * RET_QK + half] = r1.astype(BF16)
        k_s[:, hd * RET_QK + half:(hd + 1) * RET_QK] = r2.astype(BF16)
        kd_s[:, hd * RET_QK:hd * RET_QK + half] = (r1 * kdec).astype(BF16)
        kd_s[:, hd * RET_QK + half:(hd + 1) * RET_QK] = (r2 * kdec).astype(BF16)
    for piece in range(2):
        lo = piece * (hv // 2)
        v_s[:, lo:lo + hv // 2] = jnp.dot(
            h, win_ref[:, 2 * hq + lo:2 * hq + lo + hv // 2],
            preferred_element_type=F32).astype(BF16)
        g_s[:, lo:lo + hv // 2] = jnp.dot(
            h, win_ref[:, 2 * hq + hv + lo:2 * hq + hv + lo + hv // 2],
            preferred_element_type=F32).astype(BF16)

    for c in range(tm // RET_CHUNK):
        rows = slice(c * RET_CHUNK, (c + 1) * RET_CHUNK)
        for hd in range(RET_HEADS):
            qi = q_s[rows, hd * RET_QK:(hd + 1) * RET_QK]
            ki = k_s[rows, hd * RET_QK:(hd + 1) * RET_QK]
            kdi = kd_s[rows, hd * RET_QK:(hd + 1) * RET_QK]
            vi = v_s[rows, hd * RET_V:(hd + 1) * RET_V]
            state = state_s[hd]
            scores = lax.dot_general(qi, ki, (((1,), (1,)), ((), ())),
                                     preferred_element_type=F32) * intra_ref[hd]
            o = jnp.dot(scores.astype(BF16), vi, preferred_element_type=F32)
            o = o + jnp.dot(qi, state.astype(BF16), preferred_element_type=F32) * qdec_ref[hd]
            state_s[hd] = state * chunk_decay[hd] + lax.dot_general(
                kdi, vi, (((0,), (0,)), ((), ())), preferred_element_type=F32)
            o = o * lax.rsqrt(jnp.mean(o * o, axis=-1, keepdims=True) + NORM_EPS)
            gi = g_s[rows, hd * RET_V:(hd + 1) * RET_V].astype(F32)
            og_s[rows, hd * RET_V:(hd + 1) * RET_V] = (o * (gi * jax.nn.sigmoid(gi))).astype(BF16)

    o_ref[...] = x + jnp.dot(og_s[...], wout_ref[...], preferred_element_type=F32)


def _ret_tables(tm):
    h, c = RET_HEADS, RET_CHUNK
    log_gamma = jnp.log1p(-(2.0 ** (-5.0 - jnp.arange(h, dtype=F32))))
    idx = jnp.arange(c, dtype=F32)
    rel = idx[:, None] - idx[None, :]
    intra = jnp.where(rel >= 0, jnp.exp(jnp.maximum(rel, 0.0) * log_gamma[:, None, None]), 0.0)
    q_decay = jnp.exp((idx + 1.0)[None, :] * log_gamma[:, None])[:, :, None]
    k_decay = jnp.exp((c - 1.0 - idx)[None, :] * log_gamma[:, None])
    k_decay = jnp.tile(k_decay, (1, tm // c))[:, :, None]
    half = RET_QK // 2
    inv_freq = (ROPE_THETA ** (-jnp.arange(half, dtype=F32) * 2.0 / RET_QK))[None, :]
    chunk_decay = tuple(
        float(math.exp(c * math.log1p(-(2.0 ** (-5.0 - i))))) for i in range(h))
    return inv_freq, intra, q_decay, k_decay, chunk_decay


def _retention(x, pos3, gain, w_in, w_out):
    b, s, _ = x.shape
    tm = TM_RET
    inv_freq, intra, q_decay, k_decay, chunk_decay = _ret_tables(tm)
    tok = pl.BlockSpec((None, tm, D_MODEL), lambda i, j: (i, j, 0))
    return pl.pallas_call(
        functools.partial(_ret_kernel, chunk_decay=chunk_decay),
        grid=(b, s // tm),
        in_specs=[
            tok,
            pl.BlockSpec((None, tm, 1), lambda i, j: (i, j, 0)),
            _const_spec((1, D_MODEL)),
            _const_spec(w_in.shape),
            _const_spec(w_out.shape),
            _const_spec(inv_freq.shape),
            _const_spec(intra.shape),
            _const_spec(q_decay.shape),
            _const_spec(k_decay.shape),
        ],
        out_specs=tok,
        out_shape=jax.ShapeDtypeStruct(x.shape, F32),
        scratch_shapes=[
            pltpu.VMEM((tm, RET_HEADS * RET_QK), BF16),
            pltpu.VMEM((tm, RET_HEADS * RET_QK), BF16),
            pltpu.VMEM((tm, RET_HEADS * RET_QK), BF16),
            pltpu.VMEM((tm, RET_HEADS * RET_V), BF16),
            pltpu.VMEM((tm, RET_HEADS * RET_V), BF16),
            pltpu.VMEM((tm, RET_HEADS * RET_V), BF16),
            pltpu.VMEM((RET_HEADS, RET_QK, RET_V), F32),
        ],
        compiler_params=_params(2),
        name="retention",
    )(x, pos3, gain, w_in, w_out, inv_freq, intra, q_decay, k_decay)


def _lru_kernel(x_ref, gain_ref, win_ref, cw_ref, cb_ref, wri_ref, bri_ref, lam_ref,
                wout_ref, o_ref, ubuf, a_s, b_s, hprev):
    tm = x_ref.shape[0]

    @pl.when(pl.program_id(1) == 0)
    def _():
        ubuf[0:CONV_HALO, :] = jnp.zeros((CONV_HALO, LRU_W), F32)
        hprev[...] = jnp.zeros_like(hprev)

    x = x_ref[...]
    h = _rms(x, gain_ref[...]).astype(BF16)
    gate = jnp.dot(h, win_ref[:, 0:LRU_W], preferred_element_type=F32)
    ubuf[CONV_HALO:CONV_HALO + tm, :] = jnp.dot(
        h, win_ref[:, LRU_W:2 * LRU_W], preferred_element_type=F32)

    uc = cb_ref[...] + cw_ref[0:1, :] * ubuf[pl.ds(CONV_HALO - (CONV_W - 1), tm), :]
    for j in range(1, CONV_W):
        uc = uc + cw_ref[j:j + 1, :] * ubuf[pl.ds(CONV_HALO - (CONV_W - 1) + j, tm), :]
    ubuf[0:CONV_HALO, :] = ubuf[tm:tm + CONV_HALO, :]
    ucb = uc.astype(BF16)

    lam = lam_ref[...]
    neg = -lam
    softplus = jnp.maximum(neg, 0.0) + jnp.log1p(jnp.exp(-jnp.abs(neg)))
    coef = -LRU_C * softplus

    for p in range(LRU_PAIRS):
        cols = slice(p * LRU_PAIR_W, (p + 1) * LRU_PAIR_W)
        ri = jnp.dot(ucb[:, cols], wri_ref[p], preferred_element_type=F32) + bri_ref[p]
        r = jax.nn.sigmoid(ri[:, 0:LRU_PAIR_W])
        i = jax.nn.sigmoid(ri[:, LRU_PAIR_W:2 * LRU_PAIR_W])
        log_a = coef[:, cols] * r
        a_s[:, cols] = jnp.exp(log_a)
        th = jnp.tanh(log_a)
        b_s[:, cols] = jnp.sqrt(-2.0 * th / (1.0 - th)) * (i * uc[:, cols])

    a = a_s[...]
    b = b_s[...]
    row = lax.broadcasted_iota(jnp.int32, (tm, 1), 0)
    d = 1
    while d < tm:
        keep = row >= d
        a_prev = jnp.where(keep, pltpu.roll(a, d, 0), 1.0)
        b_prev = jnp.where(keep, pltpu.roll(b, d, 0), 0.0)
        b = a * b_prev + b
        a = a * a_prev
        d *= 2
    hs = b + a * hprev[...]
    hprev[...] = hs[tm - 1:tm, :]

    y = (hs * jax.nn.gelu(gate)).astype(BF16)
    o_ref[...] = x + jnp.dot(y, wout_ref[...], preferred_element_type=F32)


def _lru_gate_weights(w_r, b_r, w_i, b_i):
    bw = LRU_BLOCK_W
    z = jnp.zeros((bw, bw), w_r.dtype)
    mats, biases = [], []
    for p in range(LRU_PAIRS):
        n0, n1 = 2 * p, 2 * p + 1
        top = jnp.concatenate([w_r[n0], z, w_i[n0], z], axis=1)
        bot = jnp.concatenate([z, w_r[n1], z, w_i[n1]], axis=1)
        mats.append(jnp.concatenate([top, bot], axis=0))
        biases.append(jnp.concatenate([b_r[n0], b_r[n1], b_i[n0], b_i[n1]])[None, :])
    return jnp.stack(mats).astype(BF16), jnp.stack(biases)


def _rglru(x, gain, w_in, conv_w, conv_b, wri, bri, lam, w_out):
    b, s, _ = x.shape
    tm = TM_LRU
    tok = pl.BlockSpec((None, tm, D_MODEL), lambda i, j: (i, j, 0))
    return pl.pallas_call(
        _lru_kernel,
        grid=(b, s // tm),
        in_specs=[
            tok,
            _const_spec((1, D_MODEL)),
            _const_spec(w_in.shape),
            _const_spec(conv_w.shape),
            _const_spec(conv_b.shape),
            _const_spec(wri.shape),
            _const_spec(bri.shape),
            _const_spec(lam.shape),
            _const_spec(w_out.shape),
        ],
        out_specs=tok,
        out_shape=jax.ShapeDtypeStruct(x.shape, F32),
        scratch_shapes=[
            pltpu.VMEM((CONV_HALO + tm, LRU_W), F32),
            pltpu.VMEM((tm, LRU_W), F32),
            pltpu.VMEM((tm, LRU_W), F32),
            pltpu.VMEM((1, LRU_W), F32),
        ],
        compiler_params=_params(2),
        name="rglru",
    )(x, gain, w_in, conv_w, conv_b, wri, bri, lam, w_out)


def _swa_rope(t, cos, sin_signed, low_half):
    partner = jnp.where(low_half, pltpu.roll(t, LANES - SWA_DH // 2, 1),
                        pltpu.roll(t, SWA_DH // 2, 1))
    return t * cos + partner * sin_signed


def _swa_kernel(sinks_ref, x_ref, pos_ref, gain_ref, wqkv_ref, bqkv_ref, invf_ref,
                wout_ref, bout_ref, o_ref, q_s, kbuf, vbuf, att_s):
    tm = x_ref.shape[0]
    L = SWA_BLOCK
    nq = SWA_Q_HEADS * SWA_DH
    nkv = SWA_KV_HEADS * LANES

    @pl.when(pl.program_id(1) == 0)
    def _():
        kbuf[0:L, :] = jnp.zeros((L, nkv), BF16)
        vbuf[0:L, :] = jnp.zeros((L, nkv), BF16)

    x = x_ref[...]
    h = _rms(x, gain_ref[...]).astype(BF16)

    lane = lax.broadcasted_iota(jnp.int32, (1, LANES), 1)
    low_half = (lane % SWA_DH) < (SWA_DH // 2)
    ang = pos_ref[...].astype(F32) * invf_ref[...]
    cos = jnp.cos(ang)
    sin_signed = jnp.where(low_half, -jnp.sin(ang), jnp.sin(ang))

    q = jnp.dot(h, wqkv_ref[:, 0:nq], preferred_element_type=F32) + bqkv_ref[:, 0:nq]
    for j in range(nq // LANES):
        cols = slice(j * LANES, (j + 1) * LANES)
        q_s[:, cols] = (_swa_rope(q[:, cols], cos, sin_signed, low_half)
                        * (SWA_DH ** -0.5)).astype(BF16)
    k = jnp.dot(h, wqkv_ref[:, nq:nq + nkv], preferred_element_type=F32) + bqkv_ref[:, nq:nq + nkv]
    for j in range(nkv // LANES):
        cols = slice(j * LANES, (j + 1) * LANES)
        kbuf[L:L + tm, cols] = _swa_rope(k[:, cols], cos, sin_signed, low_half).astype(BF16)
    v = (jnp.dot(h, wqkv_ref[:, nq + nkv:nq + 2 * nkv], preferred_element_type=F32)
         + bqkv_ref[:, nq + nkv:nq + 2 * nkv])
    vbuf[L:L + tm, :] = v.astype(BF16)

    qi = lax.broadcasted_iota(jnp.int32, (2 * L, 2 * L), 0) % L
    si = lax.broadcasted_iota(jnp.int32, (2 * L, 2 * L), 1)
    dist = qi + L - si
    in_window = (dist >= 0) & (dist < SWA_WINDOW)
    first_lane_half = lax.broadcasted_iota(jnp.int32, (1, LANES), 1) < SWA_DH
    row_is_top = lax.broadcasted_iota(jnp.int32, (2 * L, 1), 0) < L
    neg_big = -1e30

    for c in range(tm // L):
        rows = slice(c * L, (c + 1) * L)
        if c == 0:
            mask = in_window & ((si >= L) | (pl.program_id(1) > 0))
        else:
            mask = in_window
        for kh in range(SWA_KV_HEADS):
            base = kh * 2 * LANES
            qst = jnp.concatenate(
                [q_s[rows, base:base + LANES], q_s[rows, base + LANES:base + 2 * LANES]], axis=0)
            kd = kbuf[c * L:c * L + 2 * L, kh * LANES:(kh + 1) * LANES]
            vd = vbuf[c * L:c * L + 2 * L, kh * LANES:(kh + 1) * LANES]
            acc = None
            for par in range(2):
                sel = first_lane_half if par == 0 else jnp.logical_not(first_lane_half)
                kp = jnp.where(sel, kd, jnp.zeros_like(kd))
                vp = jnp.where(sel, vd, jnp.zeros_like(vd))
                sink = jnp.where(row_is_top, sinks_ref[4 * kh + par], sinks_ref[4 * kh + 2 + par])
                sc = lax.dot_general(qst, kp, (((1,), (1,)), ((), ())),
                                     preferred_element_type=F32)
                sc = jnp.where(mask, sc, neg_big)
                m = jnp.maximum(jnp.max(sc, axis=-1, keepdims=True), sink)
                p = jnp.exp(sc - m)
                den = jnp.sum(p, axis=-1, keepdims=True) + jnp.exp(sink - m)
                part = jnp.dot(p.astype(BF16), vp, preferred_element_type=F32) / den
                acc = part if acc is None else acc + part
            att_s[rows, base:base + LANES] = acc[0:L].astype(BF16)
            att_s[rows, base + LANES:base + 2 * LANES] = acc[L:2 * L].astype(BF16)

    kbuf[0:L, :] = kbuf[tm:tm + L, :]
    vbuf[0:L, :] = vbuf[tm:tm + L, :]
    o_ref[...] = (x + jnp.dot(att_s[...], wout_ref[...], preferred_element_type=F32)
                  + bout_ref[...])


def _swa_weights(w_qkv, b_qkv):
    nq = SWA_Q_HEADS * SWA_DH
    nk = SWA_KV_HEADS * SWA_DH

    def dup(t):
        parts = []
        for kh in range(SWA_KV_HEADS):
            head = t[..., kh * SWA_DH:(kh + 1) * SWA_DH]
            parts += [head, head]
        return jnp.concatenate(parts, axis=-1)

    w = jnp.concatenate([w_qkv[:, :nq], dup(w_qkv[:, nq:nq + nk]), dup(w_qkv[:, nq + nk:])], axis=1)
    b = jnp.concatenate([b_qkv[:nq], dup(b_qkv[nq:nq + nk]), dup(b_qkv[nq + nk:])])[None, :]
    return w.astype(BF16), b


def _swa(x, pos3, gain, wqkv, bqkv, sinks, w_out, b_out):
    b, s, _ = x.shape
    tm = TM_SWA
    half = SWA_DH // 2
    inv_freq = ROPE_THETA ** (-jnp.arange(half, dtype=F32) * 2.0 / SWA_DH)
    inv_freq = jnp.tile(inv_freq, LANES // half)[None, :]
    tok = pl.BlockSpec((None, tm, D_MODEL), lambda i, j: (i, j, 0))
    return pl.pallas_call(
        _swa_kernel,
        grid=(b, s // tm),
        in_specs=[
            pl.BlockSpec(memory_space=pltpu.SMEM),
            tok,
            pl.BlockSpec((None, tm, 1), lambda i, j: (i, j, 0)),
            _const_spec((1, D_MODEL)),
            _const_spec(wqkv.shape),
            _const_spec(bqkv.shape),
            _const_spec(inv_freq.shape),
            _const_spec(w_out.shape),
            _const_spec((1, D_MODEL)),
        ],
        out_specs=tok,
        out_shape=jax.ShapeDtypeStruct(x.shape, F32),
        scratch_shapes=[
            pltpu.VMEM((tm, SWA_Q_HEADS * SWA_DH), BF16),
            pltpu.VMEM((SWA_BLOCK + tm, SWA_KV_HEADS * LANES), BF16),
            pltpu.VMEM((SWA_BLOCK + tm, SWA_KV_HEADS * LANES), BF16),
            pltpu.VMEM((tm, SWA_Q_HEADS * SWA_DH), BF16),
        ],
        compiler_params=_params(2),
        name="swa",
    )(sinks, x, pos3, gain, wqkv, bqkv, inv_freq, w_out, b_out)


def kernel(x, positions, mix_norm, ffn_norm, final_norm, ret_w_in, ret_w_out, lru_w_in, lru_conv_w, lru_conv_b, lru_w_r, lru_b_r, lru_w_i, lru_b_i, lru_lambda, lru_w_out, swa_w_qkv, swa_b_qkv, swa_sinks, swa_w_out, swa_b_out, ffn_w_gu, ffn_w_down):
    b, s, d = x.shape
    pos3 = positions.reshape(b, s, 1)
    fin = final_norm[None, :]
    for layer in range(DEPTH):
        kind = layer % N_MIXERS
        j = layer // N_MIXERS
        gain = mix_norm[layer][None, :]
        if kind == 0:
            x = _retention(x, pos3, gain, ret_w_in[j].astype(BF16), ret_w_out[j].astype(BF16))
        elif kind == 1:
            wri, bri = _lru_gate_weights(lru_w_r[j], lru_b_r[j], lru_w_i[j], lru_b_i[j])
            x = _rglru(x, gain, lru_w_in[j].astype(BF16), lru_conv_w[j], lru_conv_b[j][None, :],
                       wri, bri, lru_lambda[j][None, :], lru_w_out[j].astype(BF16))
        else:
            wqkv, bqkv = _swa_weights(swa_w_qkv[j], swa_b_qkv[j])
            x = _swa(x, pos3, gain, wqkv, bqkv, swa_sinks[j], swa_w_out[j].astype(BF16),
                     swa_b_out[j][None, :])
        x = _ffn(x.reshape(b * s, d), ffn_norm[layer][None, :], ffn_w_gu[layer].astype(BF16),
                 ffn_w_down[layer].astype(BF16), fin, layer == DEPTH - 1).reshape(b, s, d)
    return x
```

```python
import functools
import math

import jax
import jax.numpy as jnp
from jax import lax
from jax.experimental import pallas as pl
from jax.experimental.pallas import tpu as pltpu

F32 = jnp.float32
BF16 = jnp.bfloat16

D_MODEL = 1024
DEPTH = 4
N_MIXERS = 3
NORM_EPS = 1e-6
ROPE_THETA = 10000.0

LANES = 128
SUBLANES = 8

RET_HEADS = 4
RET_QK = 256
RET_V = 512
RET_CHUNK = 256

LRU_W = 1536
LRU_BLOCKS = 8
LRU_BLOCK_W = 192
LRU_PAIR_W = 2 * LRU_BLOCK_W
LRU_PAIRS = LRU_BLOCKS // 2
LRU_C = 8.0
CONV_W = 4
LRU_GROUP = 256
LRU_NI = LRU_GROUP // SUBLANES
LRU_TAIL = (CONV_W - 1) * SUBLANES

SWA_DH = 64
SWA_Q_HEADS = 16
SWA_KV_HEADS = 4
SWA_WINDOW = 128
SWA_BLOCK = 128
SWA_NEG = -1e30

FFN_H = 2816
FFN_CHUNK = 256

VMEM_LIMIT = 56 * 1024 * 1024

TM_FFN = 512
TM_RET = 512
TM_LRU = 512
TM_SWA = 512


def _rms(x, gain):
    return x * lax.rsqrt(jnp.mean(x * x, axis=-1, keepdims=True) + NORM_EPS) * gain


def _const_spec(shape):
    zeros = (0,) * len(shape)
    return pl.BlockSpec(shape, lambda *_: zeros, pipeline_mode=pl.Buffered(1))


def _layer_spec(stacked, layer):
    zeros = (0,) * (stacked.ndim - 1)
    return pl.BlockSpec((None,) + stacked.shape[1:], lambda *_: (layer,) + zeros,
                        pipeline_mode=pl.Buffered(1))


def _params(n_axes):
    return pltpu.CompilerParams(
        dimension_semantics=("arbitrary",) * n_axes, vmem_limit_bytes=VMEM_LIMIT)


def _ffn_kernel(x_ref, gain_ref, wgu_ref, wd_ref, fin_ref, o_ref, act_ref, *, final):
    x = x_ref[...]
    h = _rms(x, gain_ref[...]).astype(BF16)
    for c in range(FFN_H // FFN_CHUNK):
        lo = c * FFN_CHUNK
        g = jnp.dot(h, wgu_ref[:, lo:lo + FFN_CHUNK], preferred_element_type=F32)
        u = jnp.dot(h, wgu_ref[:, FFN_H + lo:FFN_H + lo + FFN_CHUNK], preferred_element_type=F32)
        act_ref[:, lo:lo + FFN_CHUNK] = (g * jax.nn.sigmoid(g) * u).astype(BF16)
    y = x + jnp.dot(act_ref[...], wd_ref[...], preferred_element_type=F32)
    if final:
        y = _rms(y, fin_ref[...])
    o_ref[...] = y


def _ffn(x2d, gain, wgu, wd, fin, layer, final):
    t = x2d.shape[0]
    tm = TM_FFN
    return pl.pallas_call(
        functools.partial(_ffn_kernel, final=final),
        grid=(t // tm,),
        in_specs=[
            pl.BlockSpec((tm, D_MODEL), lambda i: (i, 0)),
            _layer_spec(gain, layer),
            _layer_spec(wgu, layer),
            _layer_spec(wd, layer),
            _const_spec((1, D_MODEL)),
        ],
        out_specs=pl.BlockSpec((tm, D_MODEL), lambda i: (i, 0)),
        out_shape=jax.ShapeDtypeStruct(x2d.shape, F32),
        scratch_shapes=[pltpu.VMEM((tm, FFN_H), BF16)],
        compiler_params=_params(1),
        name="ffn",
    )(x2d, gain, wgu, wd, fin)


def _ret_kernel(x_ref, pos_ref, gain_ref, win_ref, wout_ref, invf_ref, intra_ref,
                qdec_ref, kdec_ref, o_ref,
                q_s, k_s, kd_s, v_s, g_s, og_s, state_s, *, chunk_decay):
    tm = x_ref.shape[0]
    hq = RET_HEADS * RET_QK
    hv = RET_HEADS * RET_V
    half = RET_QK // 2

    @pl.when(pl.program_id(1) == 0)
    def _():
        state_s[...] = jnp.zeros_like(state_s)

    x = x_ref[...]
    h = _rms(x, gain_ref[...]).astype(BF16)
    ang = pos_ref[...].astype(F32) * invf_ref[...]
    cos = jnp.cos(ang)
    sin = jnp.sin(ang)

    q = jnp.dot(h, win_ref[:, 0:hq], preferred_element_type=F32)
    for hd in range(RET_HEADS):
        t1 = q[:, hd * RET_QK:hd * RET_QK + half]
        t2 = q[:, hd * RET_QK + half:(hd + 1) * RET_QK]
        q_s[:, hd * RET_QK:hd * RET_QK + half] = (t1 * cos - t2 * sin).astype(BF16)
        q_s[:, hd * RET_QK + half:(hd + 1) * RET_QK] = (t2 * cos + t1 * sin).astype(BF16)
    k = jnp.dot(h, win_ref[:, hq:2 * hq], preferred_element_type=F32)
    scale = RET_QK ** -0.5
    for hd in range(RET_HEADS):
        t1 = k[:, hd * RET_QK:hd * RET_QK + half]
        t2 = k[:, hd * RET_QK + half:(hd + 1) * RET_QK]
        r1 = (t1 * cos - t2 * sin) * scale
        r2 = (t2 * cos + t1 * sin) * scale
        kdec = kdec_ref[hd]
        k_s[:, hd * RET_QK:hd * RET_QK + half] = r1.astype(BF16)
        k_s[:, hd * RET_QK + half:(hd + 1) * RET_QK] = r2.astype(BF16)
        kd_s[:, hd * RET_QK:hd * RET_QK + half] = (r1 * kdec).astype(BF16)
        kd_s[:, hd * RET_QK + half:(hd + 1) * RET_QK] = (r2 * kdec).astype(BF16)
    for piece in range(2):
        lo = piece * (hv // 2)
        v_s[:, lo:lo + hv // 2] = jnp.dot(
            h, win_ref[:, 2 * hq + lo:2 * hq + lo + hv // 2],
            preferred_element_type=F32).astype(BF16)
        g_s[:, lo:lo + hv // 2] = jnp.dot(
            h, win_ref[:, 2 * hq + hv + lo:2 * hq + hv + lo + hv // 2],
            preferred_element_type=F32).astype(BF16)

    for c in range(tm // RET_CHUNK):
        rows = slice(c * RET_CHUNK, (c + 1) * RET_CHUNK)
        for hd in range(RET_HEADS):
            qi = q_s[rows, hd * RET_QK:(hd + 1) * RET_QK]
            ki = k_s[rows, hd * RET_QK:(hd + 1) * RET_QK]
            kdi = kd_s[rows, hd * RET_QK:(hd + 1) * RET_QK]
            vi = v_s[rows, hd * RET_V:(hd + 1) * RET_V]
            state = state_s[hd]
            scores = lax.dot_general(qi, ki, (((1,), (1,)), ((), ())),
                                     preferred_element_type=F32) * intra_ref[hd]
            o = jnp.dot(scores.astype(BF16), vi, preferred_element_type=F32)
            o = o + jnp.dot(qi, state.astype(BF16), preferred_element_type=F32) * qdec_ref[hd]
            state_s[hd] = state * chunk_decay[hd] + lax.dot_general(
                kdi, vi, (((0,), (0,)), ((), ())), preferred_element_type=F32)
            o = o * lax.rsqrt(jnp.mean(o * o, axis=-1, keepdims=True) + NORM_EPS)
            gi = g_s[rows, hd * RET_V:(hd + 1) * RET_V].astype(F32)
            og_s[rows, hd * RET_V:(hd + 1) * RET_V] = (o * (gi * jax.nn.sigmoid(gi))).astype(BF16)

    o_ref[...] = x + jnp.dot(og_s[...], wout_ref[...], preferred_element_type=F32)


def _ret_tables(tm):
    h, c = RET_HEADS, RET_CHUNK
    log_gamma = jnp.log1p(-(2.0 ** (-5.0 - jnp.arange(h, dtype=F32))))
    idx = jnp.arange(c, dtype=F32)
    rel = idx[:, None] - idx[None, :]
    intra = jnp.where(rel >= 0, jnp.exp(jnp.maximum(rel, 0.0) * log_gamma[:, None, None]), 0.0)
    q_decay = jnp.exp((idx + 1.0)[None, :] * log_gamma[:, None])[:, :, None]
    k_decay = jnp.exp((c - 1.0 - idx)[None, :] * log_gamma[:, None])
    k_decay = jnp.tile(k_decay, (1, tm // c))[:, :, None]
    half = RET_QK // 2
    inv_freq = (ROPE_THETA ** (-jnp.arange(half, dtype=F32) * 2.0 / RET_QK))[None, :]
    chunk_decay = tuple(
        float(math.exp(c * math.log1p(-(2.0 ** (-5.0 - i))))) for i in range(h))
    return inv_freq, intra, q_decay, k_decay, chunk_decay


def _retention(x, pos3, gain, w_in, w_out, layer, j):
    b, s, _ = x.shape
    tm = TM_RET
    inv_freq, intra, q_decay, k_decay, chunk_decay = _ret_tables(tm)
    tok = pl.BlockSpec((None, tm, D_MODEL), lambda i, t: (i, t, 0))
    return pl.pallas_call(
        functools.partial(_ret_kernel, chunk_decay=chunk_decay),
        grid=(b, s // tm),
        in_specs=[
            tok,
            pl.BlockSpec((None, tm, 1), lambda i, t: (i, t, 0)),
            _layer_spec(gain, layer),
            _layer_spec(w_in, j),
            _layer_spec(w_out, j),
            _const_spec(inv_freq.shape),
            _const_spec(intra.shape),
            _const_spec(q_decay.shape),
            _const_spec(k_decay.shape),
        ],
        out_specs=tok,
        out_shape=jax.ShapeDtypeStruct(x.shape, F32),
        scratch_shapes=[
            pltpu.VMEM((tm, RET_HEADS * RET_QK), BF16),
            pltpu.VMEM((tm, RET_HEADS * RET_QK), BF16),
            pltpu.VMEM((tm, RET_HEADS * RET_QK), BF16),
            pltpu.VMEM((tm, RET_HEADS * RET_V), BF16),
            pltpu.VMEM((tm, RET_HEADS * RET_V), BF16),
            pltpu.VMEM((tm, RET_HEADS * RET_V), BF16),
            pltpu.VMEM((RET_HEADS, RET_QK, RET_V), F32),
        ],
        compiler_params=_params(2),
        name="retention",
    )(x, pos3, gain, w_in, w_out, inv_freq, intra, q_decay, k_decay)


def _lru_kernel(x_ref, gain_ref, perm_ref, permt_ref, win_ref, cw_ref, cb_ref, wri_ref, bri_ref,
                lam_ref, wout_ref, o_ref, hp_s, u_s, a_s, b_s, y_s, tail_s, hprev):
    tm = x_ref.shape[0]
    G, NI, S8 = LRU_GROUP, LRU_NI, SUBLANES
    ngroups = tm // G

    @pl.when(pl.program_id(1) == 0)
    def _():
        tail_s[...] = jnp.zeros_like(tail_s)
        hprev[...] = jnp.zeros_like(hprev)

    x = x_ref[...]
    h = _rms(x, gain_ref[...]).astype(BF16)
    for g in range(ngroups):
        rows = slice(g * G, (g + 1) * G)
        hp_s[rows, :] = jnp.dot(perm_ref[...], h[rows, :], preferred_element_type=F32).astype(BF16)
    hp = hp_s[...]
    gate = jnp.dot(hp, win_ref[:, 0:LRU_W], preferred_element_type=F32)
    u_s[...] = jnp.dot(hp, win_ref[:, LRU_W:2 * LRU_W], preferred_element_type=F32)

    sub = lax.broadcasted_iota(jnp.int32, (S8, 1), 0)
    for g in range(ngroups):
        rows = slice(g * G, (g + 1) * G)
        u_g = u_s[rows, :]
        tail_prev = tail_s[...]
        tail_cur = u_g[G - LRU_TAIL:G, :]
        heads = []
        for k in range(CONV_W - 1):
            sl = slice(k * S8, (k + 1) * S8)
            heads.append(jnp.where(sub == 0, pltpu.roll(tail_prev[sl, :], 1, 0),
                                   pltpu.roll(tail_cur[sl, :], 1, 0)))
        tail_s[...] = tail_cur
        u_ext = jnp.concatenate(heads + [u_g], axis=0)
        uc = cb_ref[...] + cw_ref[0:1, :] * u_ext[0:G, :]
        for j in range(1, CONV_W):
            uc = uc + cw_ref[j:j + 1, :] * u_ext[j * S8:j * S8 + G, :]
        u_s[rows, :] = uc

    lam = lam_ref[...]
    neg = -lam
    softplus = jnp.maximum(neg, 0.0) + jnp.log1p(jnp.exp(-jnp.abs(neg)))
    coef = -LRU_C * softplus

    uc = u_s[...]
    ucb = uc.astype(BF16)
    for p in range(LRU_PAIRS):
        cols = slice(p * LRU_PAIR_W, (p + 1) * LRU_PAIR_W)
        ri = jnp.dot(ucb[:, cols], wri_ref[p], preferred_element_type=F32) + bri_ref[p]
        r = jax.nn.sigmoid(ri[:, 0:LRU_PAIR_W])
        i = jax.nn.sigmoid(ri[:, LRU_PAIR_W:2 * LRU_PAIR_W])
        log_a = coef[:, cols] * r
        a_s[:, cols] = jnp.exp(log_a)
        th = jnp.tanh(log_a)
        b_s[:, cols] = jnp.sqrt(-2.0 * th / (1.0 - th)) * (i * uc[:, cols])

    h0 = hprev[...]
    for g in range(ngroups):
        base = g * G
        acc_a = a_s[base:base + S8, :]
        acc_b = b_s[base:base + S8, :]
        for i in range(1, NI):
            r = slice(base + i * S8, base + (i + 1) * S8)
            a_i = a_s[r, :]
            acc_b = a_i * acc_b + b_s[r, :]
            acc_a = a_i * acc_a
            a_s[r, :] = acc_a
            b_s[r, :] = acc_b
        d = 1
        while d < S8:
            keep = sub >= d
            prev_a = jnp.where(keep, pltpu.roll(acc_a, d, 0), 1.0)
            prev_b = jnp.where(keep, pltpu.roll(acc_b, d, 0), 0.0)
            acc_b = acc_a * prev_b + acc_b
            acc_a = acc_a * prev_a
            d *= 2
        h_end = acc_a * h0 + acc_b
        carry = jnp.where(sub == 0, h0, pltpu.roll(h_end, 1, 0))
        rows = slice(base, base + G)
        hs = (b_s[rows, :].reshape(NI, S8, LRU_W)
              + a_s[rows, :].reshape(NI, S8, LRU_W) * carry[None]).reshape(G, LRU_W)
        y = (hs * jax.nn.gelu(gate[rows, :])).astype(BF16)
        y_s[rows, :] = jnp.dot(permt_ref[...], y, preferred_element_type=F32).astype(BF16)
        h0 = h_end[S8 - 1:S8, :]
    hprev[...] = h0

    o_ref[...] = x + jnp.dot(y_s[...], wout_ref[...], preferred_element_type=F32)


def _lru_perm():
    p = jnp.arange(LRU_GROUP)
    time_of_row = (p % SUBLANES) * LRU_NI + p // SUBLANES
    perm = (time_of_row[:, None] == jnp.arange(LRU_GROUP)[None, :]).astype(BF16)
    return perm, perm.T


def _lru_gate_weights(w_r, b_r, w_i, b_i):
    bw = LRU_BLOCK_W
    z = jnp.zeros((bw, bw), w_r.dtype)
    mats, biases = [], []
    for p in range(LRU_PAIRS):
        n0, n1 = 2 * p, 2 * p + 1
        top = jnp.concatenate([w_r[n0], z, w_i[n0], z], axis=1)
        bot = jnp.concatenate([z, w_r[n1], z, w_i[n1]], axis=1)
        mats.append(jnp.concatenate([top, bot], axis=0))
        biases.append(jnp.concatenate([b_r[n0], b_r[n1], b_i[n0], b_i[n1]])[None, :])
    return jnp.stack(mats).astype(BF16), jnp.stack(biases)


def _rglru(x, gain, w_in, conv_w, conv_b, wri, bri, lam, w_out, layer):
    b, s, _ = x.shape
    tm = TM_LRU
    perm, permt = _lru_perm()
    tok = pl.BlockSpec((None, tm, D_MODEL), lambda i, t: (i, t, 0))
    return pl.pallas_call(
        _lru_kernel,
        grid=(b, s // tm),
        in_specs=[
            tok,
            _layer_spec(gain, layer),
            _const_spec(perm.shape),
            _const_spec(permt.shape),
            _const_spec(w_in.shape),
            _const_spec(conv_w.shape),
            _const_spec(conv_b.shape),
            _const_spec(wri.shape),
            _const_spec(bri.shape),
            _const_spec(lam.shape),
            _const_spec(w_out.shape),
        ],
        out_specs=tok,
        out_shape=jax.ShapeDtypeStruct(x.shape, F32),
        scratch_shapes=[
            pltpu.VMEM((tm, D_MODEL), BF16),
            pltpu.VMEM((tm, LRU_W), F32),
            pltpu.VMEM((tm, LRU_W), F32),
            pltpu.VMEM((tm, LRU_W), F32),
            pltpu.VMEM((tm, LRU_W), BF16),
            pltpu.VMEM((LRU_TAIL, LRU_W), F32),
            pltpu.VMEM((1, LRU_W), F32),
        ],
        compiler_params=_params(2),
        name="rglru",
    )(x, gain, perm, permt, w_in, conv_w, conv_b, wri, bri, lam, w_out)


def _swa_rope(t, cos, sin_signed, low_half):
    partner = jnp.where(low_half, pltpu.roll(t, LANES - SWA_DH // 2, 1),
                        pltpu.roll(t, SWA_DH // 2, 1))
    return t * cos + partner * sin_signed


def _swa_kernel(sinks_ref, x_ref, pos_ref, gain_ref, wqkv_ref, bqkv_ref, invf_ref,
                wout_ref, bout_ref, o_ref, q_s, kbuf, vbuf, att_s, bias_s):
    tm = x_ref.shape[0]
    L = SWA_BLOCK
    nq = SWA_Q_HEADS * SWA_DH
    nkv = SWA_KV_HEADS * LANES

    @pl.when(pl.program_id(1) == 0)
    def _():
        kbuf[0:L, :] = jnp.zeros((L, nkv), BF16)
        vbuf[0:L, :] = jnp.zeros((L, nkv), BF16)

    qi = lax.broadcasted_iota(jnp.int32, (2 * L, 4 * L), 0) % L
    si = lax.broadcasted_iota(jnp.int32, (2 * L, 4 * L), 1) % (2 * L)
    dist = qi + L - si
    in_window = (dist >= 0) & (dist < SWA_WINDOW)
    has_prev = (si >= L) | (pl.program_id(1) > 0)
    bias_s[0] = jnp.where(in_window & has_prev, 0.0, SWA_NEG)
    bias_s[1] = jnp.where(in_window, 0.0, SWA_NEG)

    x = x_ref[...]
    h = _rms(x, gain_ref[...]).astype(BF16)

    lane = lax.broadcasted_iota(jnp.int32, (1, LANES), 1)
    low_half = (lane % SWA_DH) < (SWA_DH // 2)
    ang = pos_ref[...].astype(F32) * invf_ref[...]
    cos = jnp.cos(ang)
    sin_signed = jnp.where(low_half, -jnp.sin(ang), jnp.sin(ang))

    q = jnp.dot(h, wqkv_ref[:, 0:nq], preferred_element_type=F32) + bqkv_ref[:, 0:nq]
    for j in range(nq // LANES):
        cols = slice(j * LANES, (j + 1) * LANES)
        q_s[:, cols] = (_swa_rope(q[:, cols], cos, sin_signed, low_half)
                        * (SWA_DH ** -0.5)).astype(BF16)
    k = jnp.dot(h, wqkv_ref[:, nq:nq + nkv], preferred_element_type=F32) + bqkv_ref[:, nq:nq + nkv]
    for j in range(nkv // LANES):
        cols = slice(j * LANES, (j + 1) * LANES)
        kbuf[L:L + tm, cols] = _swa_rope(k[:, cols], cos, sin_signed, low_half).astype(BF16)
    v = (jnp.dot(h, wqkv_ref[:, nq + nkv:nq + 2 * nkv], preferred_element_type=F32)
         + bqkv_ref[:, nq + nkv:nq + 2 * nkv])
    vbuf[L:L + tm, :] = v.astype(BF16)

    first_head = lane < SWA_DH
    row_is_top = lax.broadcasted_iota(jnp.int32, (2 * L, 1), 0) < L

    for c in range(tm // L):
        rows = slice(c * L, (c + 1) * L)
        bias = bias_s[0 if c == 0 else 1]
        for kh in range(SWA_KV_HEADS):
            base = kh * 2 * LANES
            qst = jnp.concatenate(
                [q_s[rows, base:base + LANES], q_s[rows, base + LANES:base + 2 * LANES]], axis=0)
            kd = kbuf[c * L:c * L + 2 * L, kh * LANES:(kh + 1) * LANES]
            vd = vbuf[c * L:c * L + 2 * L, kh * LANES:(kh + 1) * LANES]
            zero = jnp.zeros_like(kd)
            kcat = jnp.concatenate([jnp.where(first_head, kd, zero),
                                    jnp.where(first_head, zero, kd)], axis=0)
            vcat = jnp.concatenate([jnp.where(first_head, vd, zero),
                                    jnp.where(first_head, zero, vd)], axis=0)
            sc = lax.dot_general(qst, kcat, (((1,), (1,)), ((), ())),
                                 preferred_element_type=F32) + bias
            probs = []
            for par in range(2):
                sink = jnp.where(row_is_top, sinks_ref[4 * kh + par], sinks_ref[4 * kh + 2 + par])
                s_par = sc[:, par * 2 * L:(par + 1) * 2 * L]
                m = jnp.maximum(jnp.max(s_par, axis=-1, keepdims=True), sink)
                p = jnp.exp(s_par - m)
                den = jnp.sum(p, axis=-1, keepdims=True) + jnp.exp(sink - m)
                probs.append((p * (1.0 / den)).astype(BF16))
            out = jnp.dot(jnp.concatenate(probs, axis=1), vcat, preferred_element_type=F32)
            att_s[rows, base:base + LANES] = out[0:L].astype(BF16)
            att_s[rows, base + LANES:base + 2 * LANES] = out[L:2 * L].astype(BF16)

    kbuf[0:L, :] = kbuf[tm:tm + L, :]
    vbuf[0:L, :] = vbuf[tm:tm + L, :]
    o_ref[...] = (x + jnp.dot(att_s[...], wout_ref[...], preferred_element_type=F32)
                  + bout_ref[...])


def _swa_weights(w_qkv, b_qkv):
    nq = SWA_Q_HEADS * SWA_DH
    nk = SWA_KV_HEADS * SWA_DH

    def dup(t):
        parts = []
        for kh in range(SWA_KV_HEADS):
            head = t[..., kh * SWA_DH:(kh + 1) * SWA_DH]
            parts += [head, head]
        return jnp.concatenate(parts, axis=-1)

    w = jnp.concatenate([w_qkv[:, :nq], dup(w_qkv[:, nq:nq + nk]), dup(w_qkv[:, nq + nk:])], axis=1)
    b = jnp.concatenate([b_qkv[:nq], dup(b_qkv[nq:nq + nk]), dup(b_qkv[nq + nk:])])[None, :]
    return w.astype(BF16), b


def _swa(x, pos3, gain, wqkv, bqkv, sinks, w_out, b_out, layer):
    b, s, _ = x.shape
    tm = TM_SWA
    half = SWA_DH // 2
    inv_freq = ROPE_THETA ** (-jnp.arange(half, dtype=F32) * 2.0 / SWA_DH)
    inv_freq = jnp.tile(inv_freq, LANES // half)[None, :]
    tok = pl.BlockSpec((None, tm, D_MODEL), lambda i, t: (i, t, 0))
    return pl.pallas_call(
        _swa_kernel,
        grid=(b, s // tm),
        in_specs=[
            pl.BlockSpec(memory_space=pltpu.SMEM),
            tok,
            pl.BlockSpec((None, tm, 1), lambda i, t: (i, t, 0)),
            _layer_spec(gain, layer),
            _const_spec(wqkv.shape),
            _const_spec(bqkv.shape),
            _const_spec(inv_freq.shape),
            _const_spec(w_out.shape),
            _const_spec((1, D_MODEL)),
        ],
        out_specs=tok,
        out_shape=jax.ShapeDtypeStruct(x.shape, F32),
        scratch_shapes=[
            pltpu.VMEM((tm, SWA_Q_HEADS * SWA_DH), BF16),
            pltpu.VMEM((SWA_BLOCK + tm, SWA_KV_HEADS * LANES), BF16),
            pltpu.VMEM((SWA_BLOCK + tm, SWA_KV_HEADS * LANES), BF16),
            pltpu.VMEM((tm, SWA_Q_HEADS * SWA_DH), BF16),
            pltpu.VMEM((2, 2 * SWA_BLOCK, 4 * SWA_BLOCK), F32),
        ],
        compiler_params=_params(2),
        name="swa",
    )(sinks, x, pos3, gain, wqkv, bqkv, inv_freq, w_out, b_out)


def kernel(x, positions, mix_norm, ffn_norm, final_norm, ret_w_in, ret_w_out, lru_w_in, lru_conv_w, lru_conv_b, lru_w_r, lru_b_r, lru_w_i, lru_b_i, lru_lambda, lru_w_out, swa_w_qkv, swa_b_qkv, swa_sinks, swa_w_out, swa_b_out, ffn_w_gu, ffn_w_down):
    b, s, d = x.shape
    pos3 = positions.reshape(b, s, 1)
    fin = final_norm[None, :]
    mix_gain = mix_norm[:, None, :]
    ffn_gain = ffn_norm[:, None, :]
    ret_in, ret_out = ret_w_in.astype(BF16), ret_w_out.astype(BF16)
    ffn_gu, ffn_down = ffn_w_gu.astype(BF16), ffn_w_down.astype(BF16)
    for layer in range(DEPTH):
        kind = layer % N_MIXERS
        j = layer // N_MIXERS
        if kind == 0:
            x = _retention(x, pos3, mix_gain, ret_in, ret_out, layer, j)
        elif kind == 1:
            wri, bri = _lru_gate_weights(lru_w_r[j], lru_b_r[j], lru_w_i[j], lru_b_i[j])
            x = _rglru(x, mix_gain, lru_w_in[j].astype(BF16), lru_conv_w[j], lru_conv_b[j][None, :],
                       wri, bri, lru_lambda[j][None, :], lru_w_out[j].astype(BF16), layer)
        else:
            wqkv, bqkv = _swa_weights(swa_w_qkv[j], swa_b_qkv[j])
            x = _swa(x, pos3, mix_gain, wqkv, bqkv, swa_sinks[j], swa_w_out[j].astype(BF16),
                     swa_b_out[j][None, :], layer)
        x = _ffn(x.reshape(b * s, d), ffn_gain, ffn_gu, ffn_down, fin, layer,
                 layer == DEPTH - 1).reshape(b, s, d)
    return x
```

```python
import functools
import math

import jax
import jax.numpy as jnp
from jax import lax
from jax.experimental import pallas as pl
from jax.experimental.pallas import tpu as pltpu

F32 = jnp.float32
BF16 = jnp.bfloat16

D_MODEL = 1024
DEPTH = 4
N_MIXERS = 3
NORM_EPS = 1e-6
ROPE_THETA = 10000.0

LANES = 128
SUBLANES = 8

RET_HEADS = 4
RET_QK = 256
RET_V = 512
RET_CHUNK = 256

LRU_W = 1536
LRU_BLOCKS = 8
LRU_BLOCK_W = 192
LRU_PAIR_W = 2 * LRU_BLOCK_W
LRU_PAIRS = LRU_BLOCKS // 2
LRU_C = 8.0
CONV_W = 4
LRU_GROUP = 256
LRU_NI = LRU_GROUP // SUBLANES
LRU_TAIL = (CONV_W - 1) * SUBLANES

SWA_DH = 64
SWA_Q_HEADS = 16
SWA_KV_HEADS = 4
SWA_WINDOW = 128
SWA_BLOCK = 128
SWA_NEG = -1e30

FFN_H = 2816
FFN_CHUNK = 256

VMEM_LIMIT = 56 * 1024 * 1024

TM_FFN = 512
TM_RET = 512
TM_LRU = 512
TM_SWA = 512


def _rms(x, gain):
    return x * lax.rsqrt(jnp.mean(x * x, axis=-1, keepdims=True) + NORM_EPS) * gain


def _const_spec(shape):
    zeros = (0,) * len(shape)
    return pl.BlockSpec(shape, lambda *_: zeros, pipeline_mode=pl.Buffered(1))


def _layer_spec(stacked, layer):
    zeros = (0,) * (stacked.ndim - 1)
    return pl.BlockSpec((None,) + stacked.shape[1:], lambda *_: (layer,) + zeros,
                        pipeline_mode=pl.Buffered(1))


def _params(n_axes):
    return pltpu.CompilerParams(
        dimension_semantics=("arbitrary",) * n_axes, vmem_limit_bytes=VMEM_LIMIT)


def _ffn_kernel(x_ref, gain_ref, wgu_ref, wd_ref, fin_ref, o_ref, act_ref, *, final):
    x = x_ref[...]
    h = _rms(x, gain_ref[...]).astype(BF16)
    for c in range(FFN_H // FFN_CHUNK):
        lo = c * FFN_CHUNK
        g = jnp.dot(h, wgu_ref[:, lo:lo + FFN_CHUNK], preferred_element_type=F32)
        u = jnp.dot(h, wgu_ref[:, FFN_H + lo:FFN_H + lo + FFN_CHUNK], preferred_element_type=F32)
        act_ref[:, lo:lo + FFN_CHUNK] = (g * jax.nn.sigmoid(g) * u).astype(BF16)
    y = x + jnp.dot(act_ref[...], wd_ref[...], preferred_element_type=F32)
    if final:
        y = _rms(y, fin_ref[...])
    o_ref[...] = y


def _ffn(x2d, gain, wgu, wd, fin, layer, final):
    t = x2d.shape[0]
    tm = TM_FFN
    return pl.pallas_call(
        functools.partial(_ffn_kernel, final=final),
        grid=(t // tm,),
        in_specs=[
            pl.BlockSpec((tm, D_MODEL), lambda i: (i, 0)),
            _layer_spec(gain, layer),
            _layer_spec(wgu, layer),
            _layer_spec(wd, layer),
            _const_spec((1, D_MODEL)),
        ],
        out_specs=pl.BlockSpec((tm, D_MODEL), lambda i: (i, 0)),
        out_shape=jax.ShapeDtypeStruct(x2d.shape, F32),
        scratch_shapes=[pltpu.VMEM((tm, FFN_H), BF16)],
        compiler_params=_params(1),
        name="ffn",
    )(x2d, gain, wgu, wd, fin)


def _ret_kernel(x_ref, pos_ref, gain_ref, win_ref, wout_ref, invf_ref, intra_ref,
                qdec_ref, kdec_ref, o_ref,
                og_s, state_s, *, chunk_decay):
    tm = x_ref.shape[0]
    half = RET_QK // 2

    @pl.when(pl.program_id(1) == 0)
    def _():
        state_s[...] = jnp.zeros_like(state_s)

    x = x_ref[...]
    h = _rms(x, gain_ref[...]).astype(BF16)
    ang = pos_ref[...].astype(F32) * invf_ref[...]
    cos = jnp.cos(ang)
    sin = jnp.sin(ang)

    scale = RET_QK ** -0.5
    grp = 2 * RET_QK + 2 * RET_V

    def project(hd):
        return jnp.dot(h, win_ref[:, hd * grp:(hd + 1) * grp], preferred_element_type=F32)

    proj_next = project(0)
    for hd in range(RET_HEADS):
        proj = proj_next
        if hd + 1 < RET_HEADS:
            proj_next = project(hd + 1)
        q1, q2 = proj[:, 0:half], proj[:, half:RET_QK]
        k1, k2 = proj[:, RET_QK:RET_QK + half], proj[:, RET_QK + half:2 * RET_QK]
        q_rot = jnp.concatenate([q1 * cos - q2 * sin, q2 * cos + q1 * sin], axis=1).astype(BF16)
        k_rot = jnp.concatenate([k1 * cos - k2 * sin, k2 * cos + k1 * sin], axis=1) * scale
        k_dec = (k_rot * kdec_ref[hd]).astype(BF16)
        k_rot = k_rot.astype(BF16)
        v = proj[:, 2 * RET_QK:2 * RET_QK + RET_V].astype(BF16)
        g = proj[:, 2 * RET_QK + RET_V:grp]
        chunks = range(tm // RET_CHUNK)
        rows = [slice(c * RET_CHUNK, (c + 1) * RET_CHUNK) for c in chunks]
        scores = [lax.dot_general(q_rot[rows[c]], k_rot[rows[c]], (((1,), (1,)), ((), ())),
                                  preferred_element_type=F32) * intra_ref[hd] for c in chunks]
        inner = [jnp.dot(scores[c].astype(BF16), v[rows[c]], preferred_element_type=F32)
                 for c in chunks]
        incr = [lax.dot_general(k_dec[rows[c]], v[rows[c]], (((0,), (0,)), ((), ())),
                                preferred_element_type=F32) for c in chunks]
        state = state_s[hd]
        for c in chunks:
            o = inner[c] + jnp.dot(q_rot[rows[c]], state.astype(BF16),
                                   preferred_element_type=F32) * qdec_ref[hd]
            state = state * chunk_decay[hd] + incr[c]
            o = o * lax.rsqrt(jnp.mean(o * o, axis=-1, keepdims=True) + NORM_EPS)
            gi = g[rows[c]]
            og_s[rows[c], hd * RET_V:(hd + 1) * RET_V] = (
                o * (gi * jax.nn.sigmoid(gi))).astype(BF16)
        state_s[hd] = state

    o_ref[...] = x + jnp.dot(og_s[...], wout_ref[...], preferred_element_type=F32)


def _ret_tables(tm):
    h, c = RET_HEADS, RET_CHUNK
    log_gamma = jnp.log1p(-(2.0 ** (-5.0 - jnp.arange(h, dtype=F32))))
    idx = jnp.arange(c, dtype=F32)
    rel = idx[:, None] - idx[None, :]
    intra = jnp.where(rel >= 0, jnp.exp(jnp.maximum(rel, 0.0) * log_gamma[:, None, None]), 0.0)
    q_decay = jnp.exp((idx + 1.0)[None, :] * log_gamma[:, None])[:, :, None]
    k_decay = jnp.exp((c - 1.0 - idx)[None, :] * log_gamma[:, None])
    k_decay = jnp.tile(k_decay, (1, tm // c))[:, :, None]
    half = RET_QK // 2
    inv_freq = (ROPE_THETA ** (-jnp.arange(half, dtype=F32) * 2.0 / RET_QK))[None, :]
    chunk_decay = tuple(
        float(math.exp(c * math.log1p(-(2.0 ** (-5.0 - i))))) for i in range(h))
    return inv_freq, intra, q_decay, k_decay, chunk_decay


def _ret_in_weights(w_in):
    hq, hv = RET_HEADS * RET_QK, RET_HEADS * RET_V
    parts = []
    for hd in range(RET_HEADS):
        parts += [w_in[..., hd * RET_QK:(hd + 1) * RET_QK],
                  w_in[..., hq + hd * RET_QK:hq + (hd + 1) * RET_QK],
                  w_in[..., 2 * hq + hd * RET_V:2 * hq + (hd + 1) * RET_V],
                  w_in[..., 2 * hq + hv + hd * RET_V:2 * hq + hv + (hd + 1) * RET_V]]
    return jnp.concatenate(parts, axis=-1).astype(BF16)


def _retention(x, pos3, gain, w_in, w_out, layer, j):
    b, s, _ = x.shape
    tm = TM_RET
    inv_freq, intra, q_decay, k_decay, chunk_decay = _ret_tables(tm)
    tok = pl.BlockSpec((None, tm, D_MODEL), lambda i, t: (i, t, 0))
    return pl.pallas_call(
        functools.partial(_ret_kernel, chunk_decay=chunk_decay),
        grid=(b, s // tm),
        in_specs=[
            tok,
            pl.BlockSpec((None, tm, 1), lambda i, t: (i, t, 0)),
            _layer_spec(gain, layer),
            _layer_spec(w_in, j),
            _layer_spec(w_out, j),
            _const_spec(inv_freq.shape),
            _const_spec(intra.shape),
            _const_spec(q_decay.shape),
            _const_spec(k_decay.shape),
        ],
        out_specs=tok,
        out_shape=jax.ShapeDtypeStruct(x.shape, F32),
        scratch_shapes=[
            pltpu.VMEM((tm, RET_HEADS * RET_V), BF16),
            pltpu.VMEM((RET_HEADS, RET_QK, RET_V), F32),
        ],
        compiler_params=_params(2),
        name="retention",
    )(x, pos3, gain, w_in, w_out, inv_freq, intra, q_decay, k_decay)


def _lru_kernel(x_ref, gain_ref, perm_ref, permt_ref, win_ref, cw_ref, cb_ref, wri_ref, bri_ref,
                lam_ref, wout_ref, o_ref, hp_s, a_s, b_s, y_s, tail_s, hprev):
    tm = x_ref.shape[0]
    G, NI, S8 = LRU_GROUP, LRU_NI, SUBLANES
    ngroups = tm // G

    @pl.when(pl.program_id(1) == 0)
    def _():
        tail_s[...] = jnp.zeros_like(tail_s)
        hprev[...] = jnp.zeros_like(hprev)

    x = x_ref[...]
    h = _rms(x, gain_ref[...]).astype(BF16)
    for g in range(ngroups):
        rows = slice(g * G, (g + 1) * G)
        hp_s[rows, :] = jnp.dot(perm_ref[...], h[rows, :], preferred_element_type=F32).astype(BF16)
    hp = hp_s[...]
    sub = lax.broadcasted_iota(jnp.int32, (S8, 1), 0)
    neg = -lam_ref[...]
    softplus = jnp.maximum(neg, 0.0) + jnp.log1p(jnp.exp(-jnp.abs(neg)))
    coef = -LRU_C * softplus

    def project(p):
        return jnp.dot(hp, win_ref[:, 2 * p * LRU_PAIR_W:2 * (p + 1) * LRU_PAIR_W],
                       preferred_element_type=F32)

    proj_next = project(0)
    for p in range(LRU_PAIRS):
        cols = slice(p * LRU_PAIR_W, (p + 1) * LRU_PAIR_W)
        proj = proj_next
        if p + 1 < LRU_PAIRS:
            proj_next = project(p + 1)
        gate = proj[:, 0:LRU_PAIR_W]
        u = proj[:, LRU_PAIR_W:2 * LRU_PAIR_W]

        conv = []
        for g in range(ngroups):
            u_g = u[g * G:(g + 1) * G, :]
            tail_prev = tail_s[:, cols]
            tail_cur = u_g[G - LRU_TAIL:G, :]
            heads = []
            for k in range(CONV_W - 1):
                sl = slice(k * S8, (k + 1) * S8)
                heads.append(jnp.where(sub == 0, pltpu.roll(tail_prev[sl, :], 1, 0),
                                       pltpu.roll(tail_cur[sl, :], 1, 0)))
            tail_s[:, cols] = tail_cur
            u_ext = jnp.concatenate(heads + [u_g], axis=0)
            uc_g = cb_ref[:, cols] + cw_ref[0:1, cols] * u_ext[0:G, :]
            for j in range(1, CONV_W):
                uc_g = uc_g + cw_ref[j:j + 1, cols] * u_ext[j * S8:j * S8 + G, :]
            conv.append(uc_g)
        uc = jnp.concatenate(conv, axis=0)

        ri = jnp.dot(uc.astype(BF16), wri_ref[p], preferred_element_type=F32) + bri_ref[p]
        r = 0.5 * jnp.tanh(0.5 * ri[:, 0:LRU_PAIR_W]) + 0.5
        i = 0.5 * jnp.tanh(0.5 * ri[:, LRU_PAIR_W:2 * LRU_PAIR_W]) + 0.5
        log_a = coef[:, cols] * r
        a_s[:, cols] = jnp.exp(log_a)
        th = jnp.tanh(log_a)
        num = -2.0 * th
        root = jnp.where(num == 0.0, 0.0, num * lax.rsqrt(num * (1.0 - th)))
        b_s[:, cols] = root * (i * uc)

        h0 = hprev[:, cols]
        accs_a = [a_s[g * G:g * G + S8, cols] for g in range(ngroups)]
        accs_b = [b_s[g * G:g * G + S8, cols] for g in range(ngroups)]
        for i_row in range(1, NI):
            for g in range(ngroups):
                r_sl = slice(g * G + i_row * S8, g * G + (i_row + 1) * S8)
                a_i = a_s[r_sl, cols]
                accs_b[g] = a_i * accs_b[g] + b_s[r_sl, cols]
                accs_a[g] = a_i * accs_a[g]
                a_s[r_sl, cols] = accs_a[g]
                b_s[r_sl, cols] = accs_b[g]
        for g in range(ngroups):
            base = g * G
            acc_a, acc_b = accs_a[g], accs_b[g]
            d = 1
            while d < S8:
                keep = sub >= d
                prev_a = jnp.where(keep, pltpu.roll(acc_a, d, 0), 1.0)
                prev_b = jnp.where(keep, pltpu.roll(acc_b, d, 0), 0.0)
                acc_b = acc_a * prev_b + acc_b
                acc_a = acc_a * prev_a
                d *= 2
            h_end = acc_a * h0 + acc_b
            carry = jnp.where(sub == 0, h0, pltpu.roll(h_end, 1, 0))
            rows = slice(base, base + G)
            hs = (b_s[rows, cols].reshape(NI, S8, LRU_PAIR_W)
                  + a_s[rows, cols].reshape(NI, S8, LRU_PAIR_W) * carry[None]
                  ).reshape(G, LRU_PAIR_W)
            y = (hs * jax.nn.gelu(gate[rows, :])).astype(BF16)
            y_s[rows, cols] = jnp.dot(permt_ref[...], y, preferred_element_type=F32).astype(BF16)
            h0 = h_end[S8 - 1:S8, :]
        hprev[:, cols] = h0

    o_ref[...] = x + jnp.dot(y_s[...], wout_ref[...], preferred_element_type=F32)


def _lru_perm():
    p = jnp.arange(LRU_GROUP)
    time_of_row = (p % SUBLANES) * LRU_NI + p // SUBLANES
    perm = (time_of_row[:, None] == jnp.arange(LRU_GROUP)[None, :]).astype(BF16)
    return perm, perm.T


def _lru_in_weights(w_in):
    parts = []
    for p in range(LRU_PAIRS):
        lo, hi = p * LRU_PAIR_W, (p + 1) * LRU_PAIR_W
        parts += [w_in[:, lo:hi], w_in[:, LRU_W + lo:LRU_W + hi]]
    return jnp.concatenate(parts, axis=1).astype(BF16)


def _lru_gate_weights(w_r, b_r, w_i, b_i):
    bw = LRU_BLOCK_W
    z = jnp.zeros((bw, bw), w_r.dtype)
    mats, biases = [], []
    for p in range(LRU_PAIRS):
        n0, n1 = 2 * p, 2 * p + 1
        top = jnp.concatenate([w_r[n0], z, w_i[n0], z], axis=1)
        bot = jnp.concatenate([z, w_r[n1], z, w_i[n1]], axis=1)
        mats.append(jnp.concatenate([top, bot], axis=0))
        biases.append(jnp.concatenate([b_r[n0], b_r[n1], b_i[n0], b_i[n1]])[None, :])
    return jnp.stack(mats).astype(BF16), jnp.stack(biases)


def _rglru(x, gain, w_in, conv_w, conv_b, wri, bri, lam, w_out, layer):
    b, s, _ = x.shape
    tm = TM_LRU
    perm, permt = _lru_perm()
    tok = pl.BlockSpec((None, tm, D_MODEL), lambda i, t: (i, t, 0))
    return pl.pallas_call(
        _lru_kernel,
        grid=(b, s // tm),
        in_specs=[
            tok,
            _layer_spec(gain, layer),
            _const_spec(perm.shape),
            _const_spec(permt.shape),
            _const_spec(w_in.shape),
            _const_spec(conv_w.shape),
            _const_spec(conv_b.shape),
            _const_spec(wri.shape),
            _const_spec(bri.shape),
            _const_spec(lam.shape),
            _const_spec(w_out.shape),
        ],
        out_specs=tok,
        out_shape=jax.ShapeDtypeStruct(x.shape, F32),
        scratch_shapes=[
            pltpu.VMEM((tm, D_MODEL), BF16),
            pltpu.VMEM((tm, LRU_W), F32),
            pltpu.VMEM((tm, LRU_W), F32),
            pltpu.VMEM((tm, LRU_W), BF16),
            pltpu.VMEM((LRU_TAIL, LRU_W), F32),
            pltpu.VMEM((1, LRU_W), F32),
        ],
        compiler_params=_params(2),
        name="rglru",
    )(x, gain, perm, permt, w_in, conv_w, conv_b, wri, bri, lam, w_out)


def _swa_rope(t, cos, sin_signed, low_half):
    partner = jnp.where(low_half, pltpu.roll(t, LANES - SWA_DH // 2, 1),
                        pltpu.roll(t, SWA_DH // 2, 1))
    return t * cos + partner * sin_signed


def _swa_kernel(sinks_ref, x_ref, pos_ref, gain_ref, wqkv_ref, bqkv_ref, invf_ref,
                wout_ref, bout_ref, o_ref, q_s, kbuf, vbuf, att_s, bias_s):
    tm = x_ref.shape[0]
    L = SWA_BLOCK
    nkv = SWA_KV_HEADS * LANES

    @pl.when(pl.program_id(1) == 0)
    def _():
        kbuf[0:L, :] = jnp.zeros((L, nkv), BF16)
        vbuf[0:L, :] = jnp.zeros((L, nkv), BF16)

    qi = lax.broadcasted_iota(jnp.int32, (2 * L, 4 * L), 0) % L
    si = lax.broadcasted_iota(jnp.int32, (2 * L, 4 * L), 1) % (2 * L)
    dist = qi + L - si
    in_window = (dist >= 0) & (dist < SWA_WINDOW)
    has_prev = (si >= L) | (pl.program_id(1) > 0)
    bias_s[0] = jnp.where(in_window & has_prev, 0.0, SWA_NEG)
    bias_s[1] = jnp.where(in_window, 0.0, SWA_NEG)

    x = x_ref[...]
    h = _rms(x, gain_ref[...]).astype(BF16)

    lane = lax.broadcasted_iota(jnp.int32, (1, LANES), 1)
    low_half = (lane % SWA_DH) < (SWA_DH // 2)
    ang = pos_ref[...].astype(F32) * invf_ref[...]
    cos = jnp.cos(ang)
    sin_signed = jnp.where(low_half, -jnp.sin(ang), jnp.sin(ang))

    first_head = lane < SWA_DH
    row_is_top = lax.broadcasted_iota(jnp.int32, (2 * L, 1), 0) < L
    grp = 4 * LANES

    def project(kh):
        return (jnp.dot(h, wqkv_ref[:, kh * grp:(kh + 1) * grp], preferred_element_type=F32)
                + bqkv_ref[:, kh * grp:(kh + 1) * grp])

    proj_next = project(0)
    for kh in range(SWA_KV_HEADS):
        base = kh * 2 * LANES
        kcols = slice(kh * LANES, (kh + 1) * LANES)
        proj = proj_next
        if kh + 1 < SWA_KV_HEADS:
            proj_next = project(kh + 1)
        for j in range(2):
            q_s[:, base + j * LANES:base + (j + 1) * LANES] = (
                _swa_rope(proj[:, j * LANES:(j + 1) * LANES], cos, sin_signed, low_half)
                * (SWA_DH ** -0.5)).astype(BF16)
        kbuf[L:L + tm, kcols] = _swa_rope(
            proj[:, 2 * LANES:3 * LANES], cos, sin_signed, low_half).astype(BF16)
        vbuf[L:L + tm, kcols] = proj[:, 3 * LANES:4 * LANES].astype(BF16)

        blocks = range(tm // L)
        sinks = [jnp.where(row_is_top, sinks_ref[4 * kh + par], sinks_ref[4 * kh + 2 + par])
                 for par in range(2)]
        scores, vcats = [], []
        for c in blocks:
            rows = slice(c * L, (c + 1) * L)
            qst = jnp.concatenate(
                [q_s[rows, base:base + LANES], q_s[rows, base + LANES:base + 2 * LANES]], axis=0)
            kd = kbuf[c * L:c * L + 2 * L, kcols]
            vd = vbuf[c * L:c * L + 2 * L, kcols]
            zero = jnp.zeros_like(kd)
            kcat = jnp.concatenate([jnp.where(first_head, kd, zero),
                                    jnp.where(first_head, zero, kd)], axis=0)
            vcats.append(jnp.concatenate([jnp.where(first_head, vd, zero),
                                          jnp.where(first_head, zero, vd)], axis=0))
            sc = lax.dot_general(qst, kcat, (((1,), (1,)), ((), ())),
                                 preferred_element_type=F32) + bias_s[0 if c == 0 else 1]
            scores.append([sc[:, par * 2 * L:(par + 1) * 2 * L] for par in range(2)])
        maxes = [[jnp.maximum(jnp.max(scores[c][par], axis=-1, keepdims=True), sinks[par])
                  for par in range(2)] for c in blocks]
        expd = [[jnp.exp(scores[c][par] - maxes[c][par]) for par in range(2)] for c in blocks]
        dens = [[jnp.sum(expd[c][par], axis=-1, keepdims=True)
                 + jnp.exp(sinks[par] - maxes[c][par]) for par in range(2)] for c in blocks]
        for c in blocks:
            rows = slice(c * L, (c + 1) * L)
            probs = jnp.concatenate(
                [(expd[c][par] * (1.0 / dens[c][par])).astype(BF16) for par in range(2)], axis=1)
            out = jnp.dot(probs, vcats[c], preferred_element_type=F32)
            att_s[rows, base:base + LANES] = out[0:L].astype(BF16)
            att_s[rows, base + LANES:base + 2 * LANES] = out[L:2 * L].astype(BF16)

    kbuf[0:L, :] = kbuf[tm:tm + L, :]
    vbuf[0:L, :] = vbuf[tm:tm + L, :]
    o_ref[...] = (x + jnp.dot(att_s[...], wout_ref[...], preferred_element_type=F32)
                  + bout_ref[...])


def _swa_weights(w_qkv, b_qkv):
    nq = SWA_Q_HEADS * SWA_DH
    nk = SWA_KV_HEADS * SWA_DH
    group = SWA_Q_HEADS // SWA_KV_HEADS * SWA_DH

    def regroup(t):
        parts = []
        for kh in range(SWA_KV_HEADS):
            khead = t[..., nq + kh * SWA_DH:nq + (kh + 1) * SWA_DH]
            vhead = t[..., nq + nk + kh * SWA_DH:nq + nk + (kh + 1) * SWA_DH]
            parts += [t[..., kh * group:(kh + 1) * group], khead, khead, vhead, vhead]
        return jnp.concatenate(parts, axis=-1)

    return regroup(w_qkv).astype(BF16), regroup(b_qkv)[None, :]


def _swa(x, pos3, gain, wqkv, bqkv, sinks, w_out, b_out, layer):
    b, s, _ = x.shape
    tm = TM_SWA
    half = SWA_DH // 2
    inv_freq = ROPE_THETA ** (-jnp.arange(half, dtype=F32) * 2.0 / SWA_DH)
    inv_freq = jnp.tile(inv_freq, LANES // half)[None, :]
    tok = pl.BlockSpec((None, tm, D_MODEL), lambda i, t: (i, t, 0))
    return pl.pallas_call(
        _swa_kernel,
        grid=(b, s // tm),
        in_specs=[
            pl.BlockSpec(memory_space=pltpu.SMEM),
            tok,
            pl.BlockSpec((None, tm, 1), lambda i, t: (i, t, 0)),
            _layer_spec(gain, layer),
            _const_spec(wqkv.shape),
            _const_spec(bqkv.shape),
            _const_spec(inv_freq.shape),
            _const_spec(w_out.shape),
            _const_spec((1, D_MODEL)),
        ],
        out_specs=tok,
        out_shape=jax.ShapeDtypeStruct(x.shape, F32),
        scratch_shapes=[
            pltpu.VMEM((tm, SWA_Q_HEADS * SWA_DH), BF16),
            pltpu.VMEM((SWA_BLOCK + tm, SWA_KV_HEADS * LANES), BF16),
            pltpu.VMEM((SWA_BLOCK + tm, SWA_KV_HEADS * LANES), BF16),
            pltpu.VMEM((tm, SWA_Q_HEADS * SWA_DH), BF16),
            pltpu.VMEM((2, 2 * SWA_BLOCK, 4 * SWA_BLOCK), F32),
        ],
        compiler_params=_params(2),
        name="swa",
    )(sinks, x, pos3, gain, wqkv, bqkv, inv_freq, w_out, b_out)


def kernel(x, positions, mix_norm, ffn_norm, final_norm, ret_w_in, ret_w_out, lru_w_in, lru_conv_w, lru_conv_b, lru_w_r, lru_b_r, lru_w_i, lru_b_i, lru_lambda, lru_w_out, swa_w_qkv, swa_b_qkv, swa_sinks, swa_w_out, swa_b_out, ffn_w_gu, ffn_w_down):
    b, s, d = x.shape
    pos3 = positions.reshape(b, s, 1)
    fin = final_norm[None, :]
    mix_gain = mix_norm[:, None, :]
    ffn_gain = ffn_norm[:, None, :]
    ret_in, ret_out = _ret_in_weights(ret_w_in), ret_w_out.astype(BF16)
    ffn_gu, ffn_down = ffn_w_gu.astype(BF16), ffn_w_down.astype(BF16)
    for layer in range(DEPTH):
        kind = layer % N_MIXERS
        j = layer // N_MIXERS
        if kind == 0:
            x = _retention(x, pos3, mix_gain, ret_in, ret_out, layer, j)
        elif kind == 1:
            wri, bri = _lru_gate_weights(lru_w_r[j], lru_b_r[j], lru_w_i[j], lru_b_i[j])
            x = _rglru(x, mix_gain, _lru_in_weights(lru_w_in[j]), lru_conv_w[j], lru_conv_b[j][None, :],
                       wri, bri, lru_lambda[j][None, :], lru_w_out[j].astype(BF16), layer)
        else:
            wqkv, bqkv = _swa_weights(swa_w_qkv[j], swa_b_qkv[j])
            x = _swa(x, pos3, mix_gain, wqkv, bqkv, swa_sinks[j], swa_w_out[j].astype(BF16),
                     swa_b_out[j][None, :], layer)
        x = _ffn(x.reshape(b * s, d), ffn_gain, ffn_gu, ffn_down, fin, layer,
                 layer == DEPTH - 1).reshape(b, s, d)
    return x
```

```python
import functools

import jax
import jax.numpy as jnp
import numpy as np
from jax import lax
from jax.experimental import pallas as pl
from jax.experimental.pallas import tpu as pltpu

F32 = jnp.float32
BF16 = jnp.bfloat16

D_MODEL = 1024
DEPTH = 4
N_MIXERS = 3
NORM_EPS = 1e-6
ROPE_THETA = 10000.0

LANES = 128
SUBLANES = 8

RET_HEADS = 4
RET_QK = 256
RET_V = 512
RET_CHUNK = 256

LRU_W = 1536
LRU_BLOCKS = 8
LRU_BLOCK_W = 192
LRU_PAIR_W = 2 * LRU_BLOCK_W
LRU_PAIRS = LRU_BLOCKS // 2
LRU_C = 8.0
CONV_W = 4
LRU_GROUP = 256
LRU_NI = LRU_GROUP // SUBLANES
LRU_TAIL = (CONV_W - 1) * SUBLANES

SWA_DH = 64
SWA_Q_HEADS = 16
SWA_KV_HEADS = 4
SWA_WINDOW = 128
SWA_BLOCK = 128
SWA_NEG = -1e30

FFN_H = 2816
FFN_CHUNK = 256

VMEM_LIMIT = 56 * 1024 * 1024

TM_FFN = 1024
TM_RET = 512
TM_LRU = 512
TM_SWA = 512


def _rms(x, gain):
    return x * lax.rsqrt(jnp.mean(x * x, axis=-1, keepdims=True) + NORM_EPS) * gain


def _const_spec(shape):
    zeros = (0,) * len(shape)
    return pl.BlockSpec(shape, lambda *_: zeros, pipeline_mode=pl.Buffered(1))


def _layer_spec(stacked, layer):
    zeros = (0,) * (stacked.ndim - 1)
    return pl.BlockSpec((None,) + stacked.shape[1:], lambda *_: (layer,) + zeros,
                        pipeline_mode=pl.Buffered(1))


def _params(n_axes):
    return pltpu.CompilerParams(
        dimension_semantics=("arbitrary",) * n_axes, vmem_limit_bytes=VMEM_LIMIT)


class _Ffn:
    def __init__(self, x, gain_ref, wgu_ref, wd_ref, act_ref):
        self.x = x
        self.h = _rms(x, gain_ref[...]).astype(BF16)
        self.wgu_ref, self.wd_ref, self.act_ref = wgu_ref, wd_ref, act_ref
        self.done = 0

    def advance(self, n):
        for _ in range(n):
            if self.done == FFN_H // FFN_CHUNK:
                return
            lo = self.done * FFN_CHUNK
            g = jnp.dot(self.h, self.wgu_ref[:, lo:lo + FFN_CHUNK], preferred_element_type=F32)
            u = jnp.dot(self.h, self.wgu_ref[:, FFN_H + lo:FFN_H + lo + FFN_CHUNK],
                        preferred_element_type=F32)
            self.act_ref[:, lo:lo + FFN_CHUNK] = (g * jax.nn.sigmoid(g) * u).astype(BF16)
            self.done += 1

    def finish(self):
        self.advance(FFN_H // FFN_CHUNK)
        return self.x + jnp.dot(self.act_ref[...], self.wd_ref[...], preferred_element_type=F32)


def _ffn_kernel(x_ref, gain_ref, wgu_ref, wd_ref, fin_ref, o_ref, act_ref, *, final):
    y = _Ffn(x_ref[...], gain_ref, wgu_ref, wd_ref, act_ref).finish()
    if final:
        y = _rms(y, fin_ref[...])
    o_ref[...] = y


def _ffn(x2d, gain, wgu, wd, fin, layer, final):
    t = x2d.shape[0]
    tm = TM_FFN
    return pl.pallas_call(
        functools.partial(_ffn_kernel, final=final),
        grid=(t // tm,),
        in_specs=[
            pl.BlockSpec((tm, D_MODEL), lambda i: (i, 0)),
            _layer_spec(gain, layer),
            _layer_spec(wgu, layer),
            _layer_spec(wd, layer),
            _const_spec((1, D_MODEL)),
        ],
        out_specs=pl.BlockSpec((tm, D_MODEL), lambda i: (i, 0)),
        out_shape=jax.ShapeDtypeStruct(x2d.shape, F32),
        scratch_shapes=[pltpu.VMEM((tm, FFN_H), BF16)],
        compiler_params=_params(1),
        name="ffn",
    )(x2d, gain, wgu, wd, fin)


def _ret_kernel(x_ref, pos_ref, gain_ref, win_ref, wout_ref, invf_ref, intra_ref,
                qdec_ref, kdec_ref, o_ref,
                og_s, state_s, *, chunk_decay):
    tm = x_ref.shape[0]
    half = RET_QK // 2

    @pl.when(pl.program_id(1) == 0)
    def _():
        state_s[...] = jnp.zeros_like(state_s)

    x = x_ref[...]
    h = _rms(x, gain_ref[...]).astype(BF16)
    scale = RET_QK ** -0.5
    hq, hv = RET_HEADS * RET_QK, RET_HEADS * RET_V

    def project(hd):
        return [jnp.dot(h, win_ref[:, lo:lo + width], preferred_element_type=F32)
                for lo, width in ((hd * RET_QK, RET_QK), (hq + hd * RET_QK, RET_QK),
                                  (2 * hq + hd * RET_V, RET_V), (2 * hq + hv + hd * RET_V, RET_V))]

    ang = pos_ref[...].astype(F32) * invf_ref[...]
    cos = jnp.cos(ang)
    sin = jnp.sin(ang)
    proj_next = project(0)
    for hd in range(RET_HEADS):
        q, k, v, g = proj_next
        if hd + 1 < RET_HEADS:
            proj_next = project(hd + 1)
        q1, q2 = q[:, 0:half], q[:, half:RET_QK]
        k1, k2 = k[:, 0:half], k[:, half:RET_QK]
        q_rot = jnp.concatenate([q1 * cos - q2 * sin, q2 * cos + q1 * sin], axis=1).astype(BF16)
        k_rot = jnp.concatenate([k1 * cos - k2 * sin, k2 * cos + k1 * sin], axis=1) * scale
        k_dec = (k_rot * kdec_ref[hd]).astype(BF16)
        k_rot = k_rot.astype(BF16)
        v = v.astype(BF16)
        chunks = range(tm // RET_CHUNK)
        rows = [slice(c * RET_CHUNK, (c + 1) * RET_CHUNK) for c in chunks]
        scores = [lax.dot_general(q_rot[rows[c]], k_rot[rows[c]], (((1,), (1,)), ((), ())),
                                  preferred_element_type=F32) * intra_ref[hd] for c in chunks]
        inner = [jnp.dot(scores[c].astype(BF16), v[rows[c]], preferred_element_type=F32)
                 for c in chunks]
        incr = [lax.dot_general(k_dec[rows[c]], v[rows[c]], (((0,), (0,)), ((), ())),
                                preferred_element_type=F32) for c in chunks]
        state = state_s[hd]
        gated = []
        for c in chunks:
            o = inner[c] + jnp.dot(q_rot[rows[c]], state.astype(BF16),
                                   preferred_element_type=F32) * qdec_ref[hd]
            state = state * chunk_decay[hd] + incr[c]
            o = o * lax.rsqrt(jnp.mean(o * o, axis=-1, keepdims=True) + NORM_EPS)
            gi = g[rows[c]]
            gated.append((o * (gi * jax.nn.sigmoid(gi))).astype(BF16))
        state_s[hd] = state
        og_s[:, hd * RET_V:(hd + 1) * RET_V] = jnp.concatenate(gated, axis=0)

    o_ref[...] = x + jnp.dot(og_s[...], wout_ref[...], preferred_element_type=F32)


def _ret_tables(tm):
    h, c = RET_HEADS, RET_CHUNK
    log_gamma = np.log1p(-(2.0 ** (-5.0 - np.arange(h, dtype=np.float64))))
    idx = np.arange(c, dtype=np.float64)
    rel = idx[:, None] - idx[None, :]
    intra = np.where(rel >= 0, np.exp(np.maximum(rel, 0.0) * log_gamma[:, None, None]), 0.0)
    q_decay = np.exp((idx + 1.0)[None, :] * log_gamma[:, None])[:, :, None]
    k_decay = np.exp((c - 1.0 - idx)[None, :] * log_gamma[:, None])
    k_decay = np.tile(k_decay, (1, tm // c))[:, :, None]
    chunk_decay = tuple(float(np.exp(c * lg)) for lg in log_gamma)
    half = RET_QK // 2
    inv_freq = (ROPE_THETA ** (-jnp.arange(half, dtype=F32) * 2.0 / RET_QK))[None, :]
    return (inv_freq, jnp.asarray(intra, F32), jnp.asarray(q_decay, F32),
            jnp.asarray(k_decay, F32), chunk_decay)


def _retention(x, pos3, gain, w_in, w_out, layer, j):
    b, s, _ = x.shape
    tm = TM_RET
    inv_freq, intra, q_decay, k_decay, chunk_decay = _ret_tables(tm)
    tok = pl.BlockSpec((None, tm, D_MODEL), lambda i, t: (i, t, 0))
    return pl.pallas_call(
        functools.partial(_ret_kernel, chunk_decay=chunk_decay),
        grid=(b, s // tm),
        in_specs=[
            tok,
            pl.BlockSpec((None, tm, 1), lambda i, t: (i, t, 0)),
            _layer_spec(gain, layer),
            _layer_spec(w_in, j),
            _layer_spec(w_out, j),
            _const_spec(inv_freq.shape),
            _const_spec(intra.shape),
            _const_spec(q_decay.shape),
            _const_spec(k_decay.shape),
        ],
        out_specs=tok,
        out_shape=jax.ShapeDtypeStruct(x.shape, F32),
        scratch_shapes=[
            pltpu.VMEM((tm, RET_HEADS * RET_V), BF16),
            pltpu.VMEM((RET_HEADS, RET_QK, RET_V), F32),
        ],
        compiler_params=_params(2),
        name="retention",
    )(x, pos3, gain, w_in, w_out, inv_freq, intra, q_decay, k_decay)


def _lru_kernel(x_ref, gain_ref, perm_ref, permt_ref, win_ref, cw_ref, cb_ref, wri_ref, bri_ref,
                lam_ref, wout_ref, fgain_ref, wgu_ref, wd_ref, o_ref,
                hp_s, a_s, b_s, y_s, tail_s, hprev, xmid_s, act_s, *, steps_per_seq):
    tm = x_ref.shape[0]
    G, NI, S8 = LRU_GROUP, LRU_NI, SUBLANES
    ngroups = tm // G
    step = pl.program_id(0)

    @pl.when(step == 0)
    def _():
        xmid_s[...] = jnp.zeros_like(xmid_s)

    @pl.when(step % steps_per_seq == 0)
    def _():
        tail_s[...] = jnp.zeros_like(tail_s)
        hprev[...] = jnp.zeros_like(hprev)

    ffn = _Ffn(xmid_s[...], fgain_ref, wgu_ref, wd_ref, act_s)
    x = x_ref[...]
    h = _rms(x, gain_ref[...]).astype(BF16)
    ffn.advance(2)
    ffn_out = None
    for g in range(ngroups):
        rows = slice(g * G, (g + 1) * G)
        hp_s[rows, :] = jnp.dot(perm_ref[...], h[rows, :], preferred_element_type=F32).astype(BF16)
    hp = hp_s[...]
    sub = lax.broadcasted_iota(jnp.int32, (S8, 1), 0)
    neg = -lam_ref[...]
    softplus = jnp.maximum(neg, 0.0) + jnp.log1p(jnp.exp(-jnp.abs(neg)))
    coef = -LRU_C * softplus

    def project(p):
        return jnp.dot(hp, win_ref[:, 2 * p * LRU_PAIR_W:2 * (p + 1) * LRU_PAIR_W],
                       preferred_element_type=F32)

    proj_next = project(0)
    for p in range(LRU_PAIRS):
        cols = slice(p * LRU_PAIR_W, (p + 1) * LRU_PAIR_W)
        proj = proj_next
        if p + 1 < LRU_PAIRS:
            proj_next = project(p + 1)
        if p + 1 < LRU_PAIRS:
            ffn.advance((2, 3, 3)[p])
        else:
            ffn_out = ffn.finish()
        gate = proj[:, 0:LRU_PAIR_W]
        u = proj[:, LRU_PAIR_W:2 * LRU_PAIR_W]

        conv = []
        for g in range(ngroups):
            u_g = u[g * G:(g + 1) * G, :]
            tail_prev = tail_s[:, cols]
            tail_cur = u_g[G - LRU_TAIL:G, :]
            heads = []
            for k in range(CONV_W - 1):
                sl = slice(k * S8, (k + 1) * S8)
                heads.append(jnp.where(sub == 0, pltpu.roll(tail_prev[sl, :], 1, 0),
                                       pltpu.roll(tail_cur[sl, :], 1, 0)))
            tail_s[:, cols] = tail_cur
            u_ext = jnp.concatenate(heads + [u_g], axis=0)
            uc_g = cb_ref[:, cols] + cw_ref[0:1, cols] * u_ext[0:G, :]
            for j in range(1, CONV_W):
                uc_g = uc_g + cw_ref[j:j + 1, cols] * u_ext[j * S8:j * S8 + G, :]
            conv.append(uc_g)
        uc = jnp.concatenate(conv, axis=0)

        ri = jnp.dot(uc.astype(BF16), wri_ref[p], preferred_element_type=F32) + bri_ref[p]
        r = 0.5 * jnp.tanh(0.5 * ri[:, 0:LRU_PAIR_W]) + 0.5
        i = 0.5 * jnp.tanh(0.5 * ri[:, LRU_PAIR_W:2 * LRU_PAIR_W]) + 0.5
        log_a = coef[:, cols] * r
        a_s[:, cols] = jnp.exp(log_a)
        th = jnp.tanh(log_a)
        num = -2.0 * th
        root = jnp.where(num == 0.0, 0.0, num * lax.rsqrt(num * (1.0 - th)))
        b_s[:, cols] = root * (i * uc)

        h0 = hprev[:, cols]
        accs_a = [a_s[g * G:g * G + S8, cols] for g in range(ngroups)]
        accs_b = [b_s[g * G:g * G + S8, cols] for g in range(ngroups)]
        for i_row in range(1, NI):
            for g in range(ngroups):
                r_sl = slice(g * G + i_row * S8, g * G + (i_row + 1) * S8)
                a_i = a_s[r_sl, cols]
                accs_b[g] = a_i * accs_b[g] + b_s[r_sl, cols]
                accs_a[g] = a_i * accs_a[g]
                a_s[r_sl, cols] = accs_a[g]
                b_s[r_sl, cols] = accs_b[g]
        for g in range(ngroups):
            base = g * G
            acc_a, acc_b = accs_a[g], accs_b[g]
            d = 1
            while d < S8:
                keep = sub >= d
                prev_a = jnp.where(keep, pltpu.roll(acc_a, d, 0), 1.0)
                prev_b = jnp.where(keep, pltpu.roll(acc_b, d, 0), 0.0)
                acc_b = acc_a * prev_b + acc_b
                acc_a = acc_a * prev_a
                d *= 2
            h_end = acc_a * h0 + acc_b
            carry = jnp.where(sub == 0, h0, pltpu.roll(h_end, 1, 0))
            rows = slice(base, base + G)
            hs = (b_s[rows, cols].reshape(NI, S8, LRU_PAIR_W)
                  + a_s[rows, cols].reshape(NI, S8, LRU_PAIR_W) * carry[None]
                  ).reshape(G, LRU_PAIR_W)
            y = (hs * jax.nn.gelu(gate[rows, :])).astype(BF16)
            y_s[rows, cols] = jnp.dot(permt_ref[...], y, preferred_element_type=F32).astype(BF16)
            h0 = h_end[S8 - 1:S8, :]
        hprev[:, cols] = h0

    o_ref[...] = ffn_out
    xmid_s[...] = x + jnp.dot(y_s[...], wout_ref[...], preferred_element_type=F32)


def _lru_perm():
    p = np.arange(LRU_GROUP)
    time_of_row = (p % SUBLANES) * LRU_NI + p // SUBLANES
    perm = (time_of_row[:, None] == np.arange(LRU_GROUP)[None, :]).astype(np.float32)
    return jnp.asarray(perm, BF16), jnp.asarray(perm.T, BF16)


def _lru_in_weights(w_in):
    parts = []
    for p in range(LRU_PAIRS):
        lo, hi = p * LRU_PAIR_W, (p + 1) * LRU_PAIR_W
        parts += [w_in[:, lo:hi], w_in[:, LRU_W + lo:LRU_W + hi]]
    return jnp.concatenate(parts, axis=1).astype(BF16)


def _lru_gate_weights(w_r, b_r, w_i, b_i):
    bw = LRU_BLOCK_W
    z = jnp.zeros((bw, bw), w_r.dtype)
    mats, biases = [], []
    for p in range(LRU_PAIRS):
        n0, n1 = 2 * p, 2 * p + 1
        top = jnp.concatenate([w_r[n0], z, w_i[n0], z], axis=1)
        bot = jnp.concatenate([z, w_r[n1], z, w_i[n1]], axis=1)
        mats.append(jnp.concatenate([top, bot], axis=0))
        biases.append(jnp.concatenate([b_r[n0], b_r[n1], b_i[n0], b_i[n1]])[None, :])
    return jnp.stack(mats).astype(BF16), jnp.stack(biases)


def _lagged_specs(n_tiles, tm):
    tile_in = pl.BlockSpec((tm, D_MODEL), lambda i: (jnp.minimum(i, n_tiles - 1), 0))
    tile_out = pl.BlockSpec((tm, D_MODEL), lambda i: (jnp.maximum(i - 1, 0), 0))
    return tile_in, tile_out


def _rglru_ffn(x2d, seq, gain, w_in, conv_w, conv_b, wri, bri, lam, w_out, fgain, wgu, wd, layer):
    tm = TM_LRU
    n_tiles = x2d.shape[0] // tm
    perm, permt = _lru_perm()
    tile_in, tile_out = _lagged_specs(n_tiles, tm)
    return pl.pallas_call(
        functools.partial(_lru_kernel, steps_per_seq=seq // tm),
        grid=(n_tiles + 1,),
        in_specs=[
            tile_in,
            _layer_spec(gain, layer),
            _const_spec(perm.shape),
            _const_spec(permt.shape),
            _const_spec(w_in.shape),
            _const_spec(conv_w.shape),
            _const_spec(conv_b.shape),
            _const_spec(wri.shape),
            _const_spec(bri.shape),
            _const_spec(lam.shape),
            _const_spec(w_out.shape),
            _layer_spec(fgain, layer),
            _layer_spec(wgu, layer),
            _layer_spec(wd, layer),
        ],
        out_specs=tile_out,
        out_shape=jax.ShapeDtypeStruct(x2d.shape, F32),
        scratch_shapes=[
            pltpu.VMEM((tm, D_MODEL), BF16),
            pltpu.VMEM((tm, LRU_W), F32),
            pltpu.VMEM((tm, LRU_W), F32),
            pltpu.VMEM((tm, LRU_W), BF16),
            pltpu.VMEM((LRU_TAIL, LRU_W), F32),
            pltpu.VMEM((1, LRU_W), F32),
            pltpu.VMEM((tm, D_MODEL), F32),
            pltpu.VMEM((tm, FFN_H), BF16),
        ],
        compiler_params=_params(1),
        name="rglru_ffn",
    )(x2d, gain, perm, permt, w_in, conv_w, conv_b, wri, bri, lam, w_out, fgain, wgu, wd)


def _swa_rope(t, cos, sin_signed, low_half):
    partner = jnp.where(low_half, pltpu.roll(t, LANES - SWA_DH // 2, 1),
                        pltpu.roll(t, SWA_DH // 2, 1))
    return t * cos + partner * sin_signed


def _swa_kernel(sinks_ref, x_ref, pos_ref, gain_ref, wqkv_ref, bqkv_ref, invf_ref,
                wout_ref, bout_ref, fgain_ref, wgu_ref, wd_ref, o_ref,
                q_s, kbuf, vbuf, att_s, bias_s, xmid_s, act_s, *, steps_per_seq):
    tm = x_ref.shape[0]
    L = SWA_BLOCK
    nkv = SWA_KV_HEADS * LANES
    step = pl.program_id(0)
    seq_start = step % steps_per_seq == 0

    @pl.when(step == 0)
    def _():
        xmid_s[...] = jnp.zeros_like(xmid_s)

    @pl.when(seq_start)
    def _():
        kbuf[0:L, :] = jnp.zeros((L, nkv), BF16)
        vbuf[0:L, :] = jnp.zeros((L, nkv), BF16)

    ffn = _Ffn(xmid_s[...], fgain_ref, wgu_ref, wd_ref, act_s)
    x = x_ref[...]
    h = _rms(x, gain_ref[...]).astype(BF16)
    grp = 4 * LANES

    def project(kh):
        return (jnp.dot(h, wqkv_ref[:, kh * grp:(kh + 1) * grp], preferred_element_type=F32)
                + bqkv_ref[:, kh * grp:(kh + 1) * grp])

    proj_next = project(0)
    ffn.advance(3)
    ffn_out = None

    qi = lax.broadcasted_iota(jnp.int32, (2 * L, 4 * L), 0) % L
    si = lax.broadcasted_iota(jnp.int32, (2 * L, 4 * L), 1) % (2 * L)
    dist = qi + L - si
    in_window = (dist >= 0) & (dist < SWA_WINDOW)
    has_prev = (si >= L) | jnp.logical_not(seq_start)
    bias_s[0] = jnp.where(in_window & has_prev, 0.0, SWA_NEG)
    bias_s[1] = jnp.where(in_window, 0.0, SWA_NEG)

    lane = lax.broadcasted_iota(jnp.int32, (1, LANES), 1)
    low_half = (lane % SWA_DH) < (SWA_DH // 2)
    ang = pos_ref[...].astype(F32) * invf_ref[...]
    cos = jnp.cos(ang)
    sin_signed = jnp.where(low_half, -jnp.sin(ang), jnp.sin(ang))

    first_head = lane < SWA_DH
    row_is_top = lax.broadcasted_iota(jnp.int32, (2 * L, 1), 0) < L
    mixed = x + bout_ref[...]

    for kh in range(SWA_KV_HEADS):
        base = kh * 2 * LANES
        kcols = slice(kh * LANES, (kh + 1) * LANES)
        proj = proj_next
        if kh + 1 < SWA_KV_HEADS:
            proj_next = project(kh + 1)
        if kh + 1 < SWA_KV_HEADS:
            ffn.advance((2, 3, 3)[kh])
        else:
            ffn_out = ffn.finish()
        for j in range(2):
            q_s[:, base + j * LANES:base + (j + 1) * LANES] = (
                _swa_rope(proj[:, j * LANES:(j + 1) * LANES], cos, sin_signed, low_half)
                * (SWA_DH ** -0.5)).astype(BF16)
        kbuf[L:L + tm, kcols] = _swa_rope(
            proj[:, 2 * LANES:3 * LANES], cos, sin_signed, low_half).astype(BF16)
        vbuf[L:L + tm, kcols] = proj[:, 3 * LANES:4 * LANES].astype(BF16)

        blocks = range(tm // L)
        sinks = [jnp.where(row_is_top, sinks_ref[4 * kh + par], sinks_ref[4 * kh + 2 + par])
                 for par in range(2)]
        scores, vcats = [], []
        for c in blocks:
            rows = slice(c * L, (c + 1) * L)
            qst = jnp.concatenate(
                [q_s[rows, base:base + LANES], q_s[rows, base + LANES:base + 2 * LANES]], axis=0)
            kd = kbuf[c * L:c * L + 2 * L, kcols]
            vd = vbuf[c * L:c * L + 2 * L, kcols]
            zero = jnp.zeros_like(kd)
            kcat = jnp.concatenate([jnp.where(first_head, kd, zero),
                                    jnp.where(first_head, zero, kd)], axis=0)
            vcats.append(jnp.concatenate([jnp.where(first_head, vd, zero),
                                          jnp.where(first_head, zero, vd)], axis=0))
            sc = lax.dot_general(qst, kcat, (((1,), (1,)), ((), ())),
                                 preferred_element_type=F32) + bias_s[0 if c == 0 else 1]
            scores.append([sc[:, par * 2 * L:(par + 1) * 2 * L] for par in range(2)])
        maxes = [[jnp.maximum(jnp.max(scores[c][par], axis=-1, keepdims=True), sinks[par])
                  for par in range(2)] for c in blocks]
        expd = [[jnp.exp(scores[c][par] - maxes[c][par]) for par in range(2)] for c in blocks]
        dens = [[jnp.sum(expd[c][par], axis=-1, keepdims=True)
                 + jnp.exp(sinks[par] - maxes[c][par]) for par in range(2)] for c in blocks]
        for c in blocks:
            rows = slice(c * L, (c + 1) * L)
            probs = jnp.concatenate(
                [(expd[c][par] * (1.0 / dens[c][par])).astype(BF16) for par in range(2)], axis=1)
            out = jnp.dot(probs, vcats[c], preferred_element_type=F32)
            att_s[rows, base:base + LANES] = out[0:L].astype(BF16)
            att_s[rows, base + LANES:base + 2 * LANES] = out[L:2 * L].astype(BF16)
        mixed = mixed + jnp.dot(att_s[:, base:base + 2 * LANES], wout_ref[base:base + 2 * LANES, :],
                                preferred_element_type=F32)

    kbuf[0:L, :] = kbuf[tm:tm + L, :]
    vbuf[0:L, :] = vbuf[tm:tm + L, :]
    o_ref[...] = ffn_out
    xmid_s[...] = mixed


def _swa_weights(w_qkv, b_qkv):
    nq = SWA_Q_HEADS * SWA_DH
    nk = SWA_KV_HEADS * SWA_DH
    group = SWA_Q_HEADS // SWA_KV_HEADS * SWA_DH

    def regroup(t):
        parts = []
        for kh in range(SWA_KV_HEADS):
            khead = t[..., nq + kh * SWA_DH:nq + (kh + 1) * SWA_DH]
            vhead = t[..., nq + nk + kh * SWA_DH:nq + nk + (kh + 1) * SWA_DH]
            parts += [t[..., kh * group:(kh + 1) * group], khead, khead, vhead, vhead]
        return jnp.concatenate(parts, axis=-1)

    return regroup(w_qkv).astype(BF16), regroup(b_qkv)[None, :]


def _swa_ffn(x2d, pos2d, seq, gain, wqkv, bqkv, sinks, w_out, b_out, fgain, wgu, wd, layer):
    tm = TM_SWA
    n_tiles = x2d.shape[0] // tm
    half = SWA_DH // 2
    inv_freq = ROPE_THETA ** (-jnp.arange(half, dtype=F32) * 2.0 / SWA_DH)
    inv_freq = jnp.tile(inv_freq, LANES // half)[None, :]
    tile_in, tile_out = _lagged_specs(n_tiles, tm)
    return pl.pallas_call(
        functools.partial(_swa_kernel, steps_per_seq=seq // tm),
        grid=(n_tiles + 1,),
        in_specs=[
            pl.BlockSpec(memory_space=pltpu.SMEM),
            tile_in,
            pl.BlockSpec((tm, 1), lambda i: (jnp.minimum(i, n_tiles - 1), 0)),
            _layer_spec(gain, layer),
            _const_spec(wqkv.shape),
            _const_spec(bqkv.shape),
            _const_spec(inv_freq.shape),
            _const_spec(w_out.shape),
            _const_spec((1, D_MODEL)),
            _layer_spec(fgain, layer),
            _layer_spec(wgu, layer),
            _layer_spec(wd, layer),
        ],
        out_specs=tile_out,
        out_shape=jax.ShapeDtypeStruct(x2d.shape, F32),
        scratch_shapes=[
            pltpu.VMEM((tm, SWA_Q_HEADS * SWA_DH), BF16),
            pltpu.VMEM((SWA_BLOCK + tm, SWA_KV_HEADS * LANES), BF16),
            pltpu.VMEM((SWA_BLOCK + tm, SWA_KV_HEADS * LANES), BF16),
            pltpu.VMEM((tm, SWA_Q_HEADS * SWA_DH), BF16),
            pltpu.VMEM((2, 2 * SWA_BLOCK, 4 * SWA_BLOCK), F32),
            pltpu.VMEM((tm, D_MODEL), F32),
            pltpu.VMEM((tm, FFN_H), BF16),
        ],
        compiler_params=_params(1),
        name="swa_ffn",
    )(sinks, x2d, pos2d, gain, wqkv, bqkv, inv_freq, w_out, b_out, fgain, wgu, wd)


def kernel(x, positions, mix_norm, ffn_norm, final_norm, ret_w_in, ret_w_out, lru_w_in, lru_conv_w, lru_conv_b, lru_w_r, lru_b_r, lru_w_i, lru_b_i, lru_lambda, lru_w_out, swa_w_qkv, swa_b_qkv, swa_sinks, swa_w_out, swa_b_out, ffn_w_gu, ffn_w_down):
    b, s, d = x.shape
    pos3 = positions.reshape(b, s, 1)
    fin = final_norm[None, :]
    mix_gain = mix_norm[:, None, :]
    ffn_gain = ffn_norm[:, None, :]
    ret_in, ret_out = ret_w_in.astype(BF16), ret_w_out.astype(BF16)
    ffn_gu, ffn_down = ffn_w_gu.astype(BF16), ffn_w_down.astype(BF16)
    pos2d = positions.reshape(b * s, 1)
    for layer in range(DEPTH):
        kind = layer % N_MIXERS
        j = layer // N_MIXERS
        if kind == 0:
            x = _retention(x, pos3, mix_gain, ret_in, ret_out, layer, j)
            x = _ffn(x.reshape(b * s, d), ffn_gain, ffn_gu, ffn_down, fin, layer,
                     layer == DEPTH - 1).reshape(b, s, d)
        elif kind == 1:
            wri, bri = _lru_gate_weights(lru_w_r[j], lru_b_r[j], lru_w_i[j], lru_b_i[j])
            x = _rglru_ffn(x.reshape(b * s, d), s, mix_gain, _lru_in_weights(lru_w_in[j]),
                           lru_conv_w[j], lru_conv_b[j][None, :], wri, bri, lru_lambda[j][None, :],
                           lru_w_out[j].astype(BF16), ffn_gain, ffn_gu, ffn_down,
                           layer).reshape(b, s, d)
        else:
            wqkv, bqkv = _swa_weights(swa_w_qkv[j], swa_b_qkv[j])
            x = _swa_ffn(x.reshape(b * s, d), pos2d, s, mix_gain, wqkv, bqkv, swa_sinks[j],
                         swa_w_out[j].astype(BF16), swa_b_out[j][None, :], ffn_gain, ffn_gu,
                         ffn_down, layer).reshape(b, s, d)
    return x
```

```python
import functools

import jax
import jax.numpy as jnp
import numpy as np
from jax import lax
from jax.experimental import pallas as pl
from jax.experimental.pallas import tpu as pltpu

F32 = jnp.float32
BF16 = jnp.bfloat16

D_MODEL = 1024
DEPTH = 4
N_MIXERS = 3
NORM_EPS = 1e-6
ROPE_THETA = 10000.0

LANES = 128
SUBLANES = 8

RET_HEADS = 4
RET_QK = 256
RET_V = 512
RET_CHUNK = 256

LRU_W = 1536
LRU_BLOCKS = 8
LRU_BLOCK_W = 192
LRU_PAIR_W = 2 * LRU_BLOCK_W
LRU_PAIRS = LRU_BLOCKS // 2
LRU_C = 8.0
CONV_W = 4
LRU_GROUP = 256
LRU_NI = LRU_GROUP // SUBLANES
LRU_TAIL = (CONV_W - 1) * SUBLANES

SWA_DH = 64
SWA_Q_HEADS = 16
SWA_KV_HEADS = 4
SWA_WINDOW = 128
SWA_BLOCK = 128
SWA_NEG = -1e30

FFN_H = 2816
FFN_CHUNK = 256

VMEM_LIMIT = 56 * 1024 * 1024

TM_FFN = 1024
TM_RET = 512
TM_LRU = 512
TM_SWA = 512


def _rms(x, gain):
    return x * lax.rsqrt(jnp.mean(x * x, axis=-1, keepdims=True) + NORM_EPS) * gain


def _const_spec(shape):
    zeros = (0,) * len(shape)
    return pl.BlockSpec(shape, lambda *_: zeros, pipeline_mode=pl.Buffered(1))


def _layer_spec(stacked, layer):
    zeros = (0,) * (stacked.ndim - 1)
    return pl.BlockSpec((None,) + stacked.shape[1:], lambda *_: (layer,) + zeros,
                        pipeline_mode=pl.Buffered(1))


def _params(n_axes):
    return pltpu.CompilerParams(
        dimension_semantics=("arbitrary",) * n_axes, vmem_limit_bytes=VMEM_LIMIT)


class _Ffn:
    def __init__(self, x_ref, gain_ref, wgu_ref, wd_ref, act_ref):
        self.x = x_ref[...]
        self.h = _rms(self.x, gain_ref[...]).astype(BF16)
        self.wgu_ref, self.wd_ref, self.act_ref = wgu_ref, wd_ref, act_ref
        self.done = 0

    def advance(self, n):
        for _ in range(n):
            if self.done == FFN_H // FFN_CHUNK:
                return
            lo = self.done * FFN_CHUNK
            g = jnp.dot(self.h, self.wgu_ref[:, lo:lo + FFN_CHUNK], preferred_element_type=F32)
            u = jnp.dot(self.h, self.wgu_ref[:, FFN_H + lo:FFN_H + lo + FFN_CHUNK],
                        preferred_element_type=F32)
            self.act_ref[:, lo:lo + FFN_CHUNK] = (g * jax.nn.sigmoid(g) * u).astype(BF16)
            self.done += 1

    def finish(self):
        self.advance(FFN_H // FFN_CHUNK)
        return self.x + jnp.dot(self.act_ref[...], self.wd_ref[...], preferred_element_type=F32)


class _NoFfn:
    def advance(self, n):
        pass

    def finish(self):
        return None


def _lagged_steps(mixer, make_ffn, o_ref, xmid_s):
    step = pl.program_id(0)
    last = pl.num_programs(0) - 1

    @pl.when(step == 0)
    def _():
        xmid_s[...] = mixer(_NoFfn())[1]

    @pl.when((step > 0) & (step < last))
    def _():
        out, xmid = mixer(make_ffn())
        o_ref[...] = out
        xmid_s[...] = xmid

    @pl.when(step == last)
    def _():
        o_ref[...] = make_ffn().finish()


def _ffn_kernel(x_ref, gain_ref, wgu_ref, wd_ref, fin_ref, o_ref, act_ref, *, final):
    y = _Ffn(x_ref, gain_ref, wgu_ref, wd_ref, act_ref).finish()
    if final:
        y = _rms(y, fin_ref[...])
    o_ref[...] = y


def _ffn(x2d, gain, wgu, wd, fin, layer, final):
    t = x2d.shape[0]
    tm = TM_FFN
    return pl.pallas_call(
        functools.partial(_ffn_kernel, final=final),
        grid=(t // tm,),
        in_specs=[
            pl.BlockSpec((tm, D_MODEL), lambda i: (i, 0)),
            _layer_spec(gain, layer),
            _layer_spec(wgu, layer),
            _layer_spec(wd, layer),
            _const_spec((1, D_MODEL)),
        ],
        out_specs=pl.BlockSpec((tm, D_MODEL), lambda i: (i, 0)),
        out_shape=jax.ShapeDtypeStruct(x2d.shape, F32),
        scratch_shapes=[pltpu.VMEM((tm, FFN_H), BF16)],
        compiler_params=_params(1),
        name="ffn",
    )(x2d, gain, wgu, wd, fin)


def _ret_kernel(*refs, chunk_decay, emit_tables):
    if emit_tables:
        (x_ref, pos_ref, invf_ref, gain_ref, win_ref, wout_ref, intra_ref, qdec_ref, kdec_ref,
         o_ref, cos_ref, sin_ref, og_s, state_s) = refs
    else:
        (x_ref, cos_ref, sin_ref, gain_ref, win_ref, wout_ref, intra_ref, qdec_ref, kdec_ref,
         o_ref, og_s, state_s) = refs
    tm = x_ref.shape[0]
    half = RET_QK // 2

    @pl.when(pl.program_id(1) == 0)
    def _():
        state_s[...] = jnp.zeros_like(state_s)

    x = x_ref[...]
    h = _rms(x, gain_ref[...]).astype(BF16)
    scale = RET_QK ** -0.5
    hq, hv = RET_HEADS * RET_QK, RET_HEADS * RET_V

    def project(hd):
        return [jnp.dot(h, win_ref[:, lo:lo + width], preferred_element_type=F32)
                for lo, width in ((hd * RET_QK, RET_QK), (hq + hd * RET_QK, RET_QK),
                                  (2 * hq + hd * RET_V, RET_V), (2 * hq + hv + hd * RET_V, RET_V))]

    if emit_tables:
        ang = pos_ref[...].astype(F32) * invf_ref[...]
        cos = jnp.cos(ang)
        sin = jnp.sin(ang)
        cos_ref[...] = cos
        sin_ref[...] = sin
    else:
        cos = cos_ref[...]
        sin = sin_ref[...]
    proj_next = project(0)
    for hd in range(RET_HEADS):
        q, k, v, g = proj_next
        if hd + 1 < RET_HEADS:
            proj_next = project(hd + 1)
        q1, q2 = q[:, 0:half], q[:, half:RET_QK]
        k1, k2 = k[:, 0:half], k[:, half:RET_QK]
        q_rot = jnp.concatenate([q1 * cos - q2 * sin, q2 * cos + q1 * sin], axis=1).astype(BF16)
        k_rot = jnp.concatenate([k1 * cos - k2 * sin, k2 * cos + k1 * sin], axis=1) * scale
        k_dec = (k_rot * kdec_ref[hd]).astype(BF16)
        k_rot = k_rot.astype(BF16)
        v = v.astype(BF16)
        chunks = range(tm // RET_CHUNK)
        rows = [slice(c * RET_CHUNK, (c + 1) * RET_CHUNK) for c in chunks]
        scores = [lax.dot_general(q_rot[rows[c]], k_rot[rows[c]], (((1,), (1,)), ((), ())),
                                  preferred_element_type=F32) * intra_ref[hd] for c in chunks]
        inner = [jnp.dot(scores[c].astype(BF16), v[rows[c]], preferred_element_type=F32)
                 for c in chunks]
        incr = [lax.dot_general(k_dec[rows[c]], v[rows[c]], (((0,), (0,)), ((), ())),
                                preferred_element_type=F32) for c in chunks]
        state = state_s[hd]
        gated = []
        for c in chunks:
            o = inner[c] + jnp.dot(q_rot[rows[c]], state.astype(BF16),
                                   preferred_element_type=F32) * qdec_ref[hd]
            state = state * chunk_decay[hd] + incr[c]
            o = o * lax.rsqrt(jnp.mean(o * o, axis=-1, keepdims=True) + NORM_EPS)
            gi = g[rows[c]]
            gated.append((o * (gi * jax.nn.sigmoid(gi))).astype(BF16))
        state_s[hd] = state
        og_s[:, hd * RET_V:(hd + 1) * RET_V] = jnp.concatenate(gated, axis=0)

    o_ref[...] = x + jnp.dot(og_s[...], wout_ref[...], preferred_element_type=F32)


def _ret_tables(tm):
    h, c = RET_HEADS, RET_CHUNK
    log_gamma = np.log1p(-(2.0 ** (-5.0 - np.arange(h, dtype=np.float64))))
    idx = np.arange(c, dtype=np.float64)
    rel = idx[:, None] - idx[None, :]
    intra = np.where(rel >= 0, np.exp(np.maximum(rel, 0.0) * log_gamma[:, None, None]), 0.0)
    q_decay = np.exp((idx + 1.0)[None, :] * log_gamma[:, None])[:, :, None]
    k_decay = np.exp((c - 1.0 - idx)[None, :] * log_gamma[:, None])
    k_decay = np.tile(k_decay, (1, tm // c))[:, :, None]
    chunk_decay = tuple(float(np.exp(c * lg)) for lg in log_gamma)
    half = RET_QK // 2
    inv_freq = (ROPE_THETA ** (-jnp.arange(half, dtype=F32) * 2.0 / RET_QK))[None, :]
    return (inv_freq, jnp.asarray(intra, F32), jnp.asarray(q_decay, F32),
            jnp.asarray(k_decay, F32), chunk_decay)


def _retention(x, rope, gain, w_in, w_out, layer, j):
    b, s, _ = x.shape
    tm = TM_RET
    inv_freq, intra, q_decay, k_decay, chunk_decay = _ret_tables(tm)
    emit_tables = not isinstance(rope, tuple)
    tok = pl.BlockSpec((None, tm, D_MODEL), lambda i, t: (i, t, 0))
    table = pl.BlockSpec((None, tm, LANES), lambda i, t: (i, t, 0))
    table_shape = jax.ShapeDtypeStruct((b, s, LANES), F32)
    if emit_tables:
        rope_args = (rope, inv_freq)
        rope_specs = [pl.BlockSpec((None, tm, 1), lambda i, t: (i, t, 0)),
                      _const_spec(inv_freq.shape)]
        out_specs, out_shape = [tok, table, table], [jax.ShapeDtypeStruct(x.shape, F32),
                                                     table_shape, table_shape]
    else:
        rope_args, rope_specs = rope, [table, table]
        out_specs, out_shape = tok, jax.ShapeDtypeStruct(x.shape, F32)
    return pl.pallas_call(
        functools.partial(_ret_kernel, chunk_decay=chunk_decay, emit_tables=emit_tables),
        grid=(b, s // tm),
        in_specs=[
            tok,
            *rope_specs,
            _layer_spec(gain, layer),
            _layer_spec(w_in, j),
            _layer_spec(w_out, j),
            _const_spec(intra.shape),
            _const_spec(q_decay.shape),
            _const_spec(k_decay.shape),
        ],
        out_specs=out_specs,
        out_shape=out_shape,
        scratch_shapes=[
            pltpu.VMEM((tm, RET_HEADS * RET_V), BF16),
            pltpu.VMEM((RET_HEADS, RET_QK, RET_V), F32),
        ],
        compiler_params=_params(2),
        name="retention",
    )(x, *rope_args, gain, w_in, w_out, intra, q_decay, k_decay)


def _lru_kernel(x_ref, gain_ref, perm_ref, permt_ref, win_ref, cw_ref, cb_ref, wri_ref, bri_ref,
                lam_ref, wout_ref, fgain_ref, wgu_ref, wd_ref, o_ref,
                hp_s, a_s, b_s, y_s, tail_s, hprev, xmid_s, act_s, *, steps_per_seq):
    @pl.when(pl.program_id(0) % steps_per_seq == 0)
    def _():
        tail_s[...] = jnp.zeros_like(tail_s)
        hprev[...] = jnp.zeros_like(hprev)

    mixer = functools.partial(
        _lru_mixer, x_ref=x_ref, gain_ref=gain_ref, perm_ref=perm_ref, permt_ref=permt_ref,
        win_ref=win_ref, cw_ref=cw_ref, cb_ref=cb_ref, wri_ref=wri_ref, bri_ref=bri_ref,
        lam_ref=lam_ref, wout_ref=wout_ref, hp_s=hp_s, a_s=a_s, b_s=b_s, y_s=y_s, tail_s=tail_s,
        hprev=hprev)
    make_ffn = functools.partial(_Ffn, xmid_s, fgain_ref, wgu_ref, wd_ref, act_s)
    _lagged_steps(mixer, make_ffn, o_ref, xmid_s)


def _lru_mixer(ffn, *, x_ref, gain_ref, perm_ref, permt_ref, win_ref, cw_ref, cb_ref, wri_ref,
               bri_ref, lam_ref, wout_ref, hp_s, a_s, b_s, y_s, tail_s, hprev):
    tm = x_ref.shape[0]
    G, NI, S8 = LRU_GROUP, LRU_NI, SUBLANES
    ngroups = tm // G
    x = x_ref[...]
    h = _rms(x, gain_ref[...]).astype(BF16)
    ffn.advance(2)
    ffn_out = None
    for g in range(ngroups):
        rows = slice(g * G, (g + 1) * G)
        hp_s[rows, :] = jnp.dot(perm_ref[...], h[rows, :], preferred_element_type=F32).astype(BF16)
    hp = hp_s[...]
    sub = lax.broadcasted_iota(jnp.int32, (S8, 1), 0)
    neg = -lam_ref[...]
    softplus = jnp.maximum(neg, 0.0) + jnp.log1p(jnp.exp(-jnp.abs(neg)))
    coef = -LRU_C * softplus

    def project(p):
        return jnp.dot(hp, win_ref[:, 2 * p * LRU_PAIR_W:2 * (p + 1) * LRU_PAIR_W],
                       preferred_element_type=F32)

    proj_next = project(0)
    for p in range(LRU_PAIRS):
        cols = slice(p * LRU_PAIR_W, (p + 1) * LRU_PAIR_W)
        proj = proj_next
        if p + 1 < LRU_PAIRS:
            proj_next = project(p + 1)
        if p + 1 < LRU_PAIRS:
            ffn.advance((2, 3, 3)[p])
        else:
            ffn_out = ffn.finish()
        gate = proj[:, 0:LRU_PAIR_W]
        u = proj[:, LRU_PAIR_W:2 * LRU_PAIR_W]

        conv = []
        for g in range(ngroups):
            u_g = u[g * G:(g + 1) * G, :]
            tail_prev = tail_s[:, cols]
            tail_cur = u_g[G - LRU_TAIL:G, :]
            heads = []
            for k in range(CONV_W - 1):
                sl = slice(k * S8, (k + 1) * S8)
                heads.append(jnp.where(sub == 0, pltpu.roll(tail_prev[sl, :], 1, 0),
                                       pltpu.roll(tail_cur[sl, :], 1, 0)))
            tail_s[:, cols] = tail_cur
            u_ext = jnp.concatenate(heads + [u_g], axis=0)
            uc_g = cb_ref[:, cols] + cw_ref[0:1, cols] * u_ext[0:G, :]
            for j in range(1, CONV_W):
                uc_g = uc_g + cw_ref[j:j + 1, cols] * u_ext[j * S8:j * S8 + G, :]
            conv.append(uc_g)
        uc = jnp.concatenate(conv, axis=0)

        ri = jnp.dot(uc.astype(BF16), wri_ref[p], preferred_element_type=F32) + bri_ref[p]
        r = 0.5 * jnp.tanh(0.5 * ri[:, 0:LRU_PAIR_W]) + 0.5
        i = 0.5 * jnp.tanh(0.5 * ri[:, LRU_PAIR_W:2 * LRU_PAIR_W]) + 0.5
        log_a = coef[:, cols] * r
        a_s[:, cols] = jnp.exp(log_a)
        th = jnp.tanh(log_a)
        num = -2.0 * th
        root = jnp.where(num == 0.0, 0.0, num * lax.rsqrt(num * (1.0 - th)))
        b_s[:, cols] = root * (i * uc)

        h0 = hprev[:, cols]
        accs_a = [a_s[g * G:g * G + S8, cols] for g in range(ngroups)]
        accs_b = [b_s[g * G:g * G + S8, cols] for g in range(ngroups)]
        for i_row in range(1, NI):
            for g in range(ngroups):
                r_sl = slice(g * G + i_row * S8, g * G + (i_row + 1) * S8)
                a_i = a_s[r_sl, cols]
                accs_b[g] = a_i * accs_b[g] + b_s[r_sl, cols]
                accs_a[g] = a_i * accs_a[g]
                a_s[r_sl, cols] = accs_a[g]
                b_s[r_sl, cols] = accs_b[g]
        for g in range(ngroups):
            base = g * G
            acc_a, acc_b = accs_a[g], accs_b[g]
            d = 1
            while d < S8:
                keep = sub >= d
                prev_a = jnp.where(keep, pltpu.roll(acc_a, d, 0), 1.0)
                prev_b = jnp.where(keep, pltpu.roll(acc_b, d, 0), 0.0)
                acc_b = acc_a * prev_b + acc_b
                acc_a = acc_a * prev_a
                d *= 2
            h_end = acc_a * h0 + acc_b
            carry = jnp.where(sub == 0, h0, pltpu.roll(h_end, 1, 0))
            rows = slice(base, base + G)
            hs = (b_s[rows, cols].reshape(NI, S8, LRU_PAIR_W)
                  + a_s[rows, cols].reshape(NI, S8, LRU_PAIR_W) * carry[None]
                  ).reshape(G, LRU_PAIR_W)
            y = (hs * jax.nn.gelu(gate[rows, :])).astype(BF16)
            y_s[rows, cols] = jnp.dot(permt_ref[...], y, preferred_element_type=F32).astype(BF16)
            h0 = h_end[S8 - 1:S8, :]
        hprev[:, cols] = h0

    return ffn_out, x + jnp.dot(y_s[...], wout_ref[...], preferred_element_type=F32)


def _lru_perm():
    p = np.arange(LRU_GROUP)
    time_of_row = (p % SUBLANES) * LRU_NI + p // SUBLANES
    perm = (time_of_row[:, None] == np.arange(LRU_GROUP)[None, :]).astype(np.float32)
    return jnp.asarray(perm, BF16), jnp.asarray(perm.T, BF16)


def _lru_in_weights(w_in):
    parts = []
    for p in range(LRU_PAIRS):
        lo, hi = p * LRU_PAIR_W, (p + 1) * LRU_PAIR_W
        parts += [w_in[:, lo:hi], w_in[:, LRU_W + lo:LRU_W + hi]]
    return jnp.concatenate(parts, axis=1).astype(BF16)


def _lru_gate_weights(w_r, b_r, w_i, b_i):
    bw = LRU_BLOCK_W
    z = jnp.zeros((bw, bw), w_r.dtype)
    mats, biases = [], []
    for p in range(LRU_PAIRS):
        n0, n1 = 2 * p, 2 * p + 1
        top = jnp.concatenate([w_r[n0], z, w_i[n0], z], axis=1)
        bot = jnp.concatenate([z, w_r[n1], z, w_i[n1]], axis=1)
        mats.append(jnp.concatenate([top, bot], axis=0))
        biases.append(jnp.concatenate([b_r[n0], b_r[n1], b_i[n0], b_i[n1]])[None, :])
    return jnp.stack(mats).astype(BF16), jnp.stack(biases)


def _lagged_specs(n_tiles, tm):
    tile_in = pl.BlockSpec((tm, D_MODEL), lambda i: (jnp.minimum(i, n_tiles - 1), 0))
    tile_out = pl.BlockSpec((tm, D_MODEL), lambda i: (jnp.maximum(i - 1, 0), 0))
    return tile_in, tile_out


def _rglru_ffn(x2d, seq, gain, w_in, conv_w, conv_b, wri, bri, lam, w_out, fgain, wgu, wd, layer):
    tm = TM_LRU
    n_tiles = x2d.shape[0] // tm
    perm, permt = _lru_perm()
    tile_in, tile_out = _lagged_specs(n_tiles, tm)
    return pl.pallas_call(
        functools.partial(_lru_kernel, steps_per_seq=seq // tm),
        grid=(n_tiles + 1,),
        in_specs=[
            tile_in,
            _layer_spec(gain, layer),
            _const_spec(perm.shape),
            _const_spec(permt.shape),
            _const_spec(w_in.shape),
            _const_spec(conv_w.shape),
            _const_spec(conv_b.shape),
            _const_spec(wri.shape),
            _const_spec(bri.shape),
            _const_spec(lam.shape),
            _const_spec(w_out.shape),
            _layer_spec(fgain, layer),
            _layer_spec(wgu, layer),
            _layer_spec(wd, layer),
        ],
        out_specs=tile_out,
        out_shape=jax.ShapeDtypeStruct(x2d.shape, F32),
        scratch_shapes=[
            pltpu.VMEM((tm, D_MODEL), BF16),
            pltpu.VMEM((tm, LRU_W), F32),
            pltpu.VMEM((tm, LRU_W), F32),
            pltpu.VMEM((tm, LRU_W), BF16),
            pltpu.VMEM((LRU_TAIL, LRU_W), F32),
            pltpu.VMEM((1, LRU_W), F32),
            pltpu.VMEM((tm, D_MODEL), F32),
            pltpu.VMEM((tm, FFN_H), BF16),
        ],
        compiler_params=_params(1),
        name="rglru_ffn",
    )(x2d, gain, perm, permt, w_in, conv_w, conv_b, wri, bri, lam, w_out, fgain, wgu, wd)


def _swa_rope(t, cos, sin_signed, low_half):
    partner = jnp.where(low_half, pltpu.roll(t, LANES - SWA_DH // 2, 1),
                        pltpu.roll(t, SWA_DH // 2, 1))
    return t * cos + partner * sin_signed


def _swa_kernel(sinks_ref, x_ref, cos_ref, sin_ref, gain_ref, wqkv_ref, bqkv_ref, sel_ref,
                wout_ref, bout_ref, fgain_ref, wgu_ref, wd_ref, o_ref,
                q_s, kbuf, vbuf, att_s, bias_s, xmid_s, act_s, *, steps_per_seq):
    seq_start = pl.program_id(0) % steps_per_seq == 0

    @pl.when(seq_start)
    def _():
        halo = (SWA_BLOCK, SWA_KV_HEADS * LANES)
        kbuf[0:SWA_BLOCK, :] = jnp.zeros(halo, BF16)
        vbuf[0:SWA_BLOCK, :] = jnp.zeros(halo, BF16)

    mixer = functools.partial(
        _swa_mixer, seq_start=seq_start, sinks_ref=sinks_ref, x_ref=x_ref, cos_ref=cos_ref,
        sin_ref=sin_ref, gain_ref=gain_ref, wqkv_ref=wqkv_ref, bqkv_ref=bqkv_ref, sel_ref=sel_ref,
        wout_ref=wout_ref, bout_ref=bout_ref, q_s=q_s, kbuf=kbuf, vbuf=vbuf, att_s=att_s,
        bias_s=bias_s)
    make_ffn = functools.partial(_Ffn, xmid_s, fgain_ref, wgu_ref, wd_ref, act_s)
    _lagged_steps(mixer, make_ffn, o_ref, xmid_s)


def _swa_mixer(ffn, *, seq_start, sinks_ref, x_ref, cos_ref, sin_ref, gain_ref, wqkv_ref, bqkv_ref,
               sel_ref, wout_ref, bout_ref, q_s, kbuf, vbuf, att_s, bias_s):
    tm = x_ref.shape[0]
    L = SWA_BLOCK
    x = x_ref[...]
    h = _rms(x, gain_ref[...]).astype(BF16)
    grp = 4 * LANES

    def project(kh):
        return (jnp.dot(h, wqkv_ref[:, kh * grp:(kh + 1) * grp], preferred_element_type=F32)
                + bqkv_ref[:, kh * grp:(kh + 1) * grp])

    proj_next = project(0)
    ffn.advance(3)
    ffn_out = None

    qi = lax.broadcasted_iota(jnp.int32, (2 * L, 4 * L), 0) % L
    si = lax.broadcasted_iota(jnp.int32, (2 * L, 4 * L), 1) % (2 * L)
    dist = qi + L - si
    in_window = (dist >= 0) & (dist < SWA_WINDOW)
    has_prev = (si >= L) | jnp.logical_not(seq_start)
    bias_s[0] = jnp.where(in_window & has_prev, 0.0, SWA_NEG)
    bias_s[1] = jnp.where(in_window, 0.0, SWA_NEG)

    lane = lax.broadcasted_iota(jnp.int32, (1, LANES), 1)
    low_half = (lane % SWA_DH) < (SWA_DH // 2)
    cos = jnp.dot(cos_ref[...], sel_ref[0], precision=lax.Precision.HIGHEST,
                  preferred_element_type=F32)
    sin_signed = jnp.dot(sin_ref[...], sel_ref[1], precision=lax.Precision.HIGHEST,
                         preferred_element_type=F32)

    first_head = lane < SWA_DH
    row_is_top = lax.broadcasted_iota(jnp.int32, (2 * L, 1), 0) < L
    mixed = x + bout_ref[...]

    for kh in range(SWA_KV_HEADS):
        base = kh * 2 * LANES
        kcols = slice(kh * LANES, (kh + 1) * LANES)
        proj = proj_next
        if kh + 1 < SWA_KV_HEADS:
            proj_next = project(kh + 1)
        if kh + 1 < SWA_KV_HEADS:
            ffn.advance((2, 3, 3)[kh])
        else:
            ffn_out = ffn.finish()
        for j in range(2):
            q_s[:, base + j * LANES:base + (j + 1) * LANES] = (
                _swa_rope(proj[:, j * LANES:(j + 1) * LANES], cos, sin_signed, low_half)
                * (SWA_DH ** -0.5)).astype(BF16)
        kbuf[L:L + tm, kcols] = _swa_rope(
            proj[:, 2 * LANES:3 * LANES], cos, sin_signed, low_half).astype(BF16)
        vbuf[L:L + tm, kcols] = proj[:, 3 * LANES:4 * LANES].astype(BF16)

        blocks = range(tm // L)
        sinks = [jnp.where(row_is_top, sinks_ref[4 * kh + par], sinks_ref[4 * kh + 2 + par])
                 for par in range(2)]
        scores, vcats = [], []
        for c in blocks:
            rows = slice(c * L, (c + 1) * L)
            qst = jnp.concatenate(
                [q_s[rows, base:base + LANES], q_s[rows, base + LANES:base + 2 * LANES]], axis=0)
            kd = kbuf[c * L:c * L + 2 * L, kcols]
            vd = vbuf[c * L:c * L + 2 * L, kcols]
            zero = jnp.zeros_like(kd)
            kcat = jnp.concatenate([jnp.where(first_head, kd, zero),
                                    jnp.where(first_head, zero, kd)], axis=0)
            vcats.append(jnp.concatenate([jnp.where(first_head, vd, zero),
                                          jnp.where(first_head, zero, vd)], axis=0))
            sc = lax.dot_general(qst, kcat, (((1,), (1,)), ((), ())),
                                 preferred_element_type=F32) + bias_s[0 if c == 0 else 1]
            scores.append([sc[:, par * 2 * L:(par + 1) * 2 * L] for par in range(2)])
        maxes = [[jnp.maximum(jnp.max(scores[c][par], axis=-1, keepdims=True), sinks[par])
                  for par in range(2)] for c in blocks]
        expd = [[jnp.exp(scores[c][par] - maxes[c][par]) for par in range(2)] for c in blocks]
        dens = [[jnp.sum(expd[c][par], axis=-1, keepdims=True)
                 + jnp.exp(sinks[par] - maxes[c][par]) for par in range(2)] for c in blocks]
        for c in blocks:
            rows = slice(c * L, (c + 1) * L)
            probs = jnp.concatenate(
                [(expd[c][par] * (1.0 / dens[c][par])).astype(BF16) for par in range(2)], axis=1)
            out = jnp.dot(probs, vcats[c], preferred_element_type=F32)
            att_s[rows, base:base + LANES] = out[0:L].astype(BF16)
            att_s[rows, base + LANES:base + 2 * LANES] = out[L:2 * L].astype(BF16)
        mixed = mixed + jnp.dot(att_s[:, base:base + 2 * LANES], wout_ref[base:base + 2 * LANES, :],
                                preferred_element_type=F32)

    kbuf[0:L, :] = kbuf[tm:tm + L, :]
    vbuf[0:L, :] = vbuf[tm:tm + L, :]
    return ffn_out, mixed


def _swa_weights(w_qkv, b_qkv):
    nq = SWA_Q_HEADS * SWA_DH
    nk = SWA_KV_HEADS * SWA_DH
    group = SWA_Q_HEADS // SWA_KV_HEADS * SWA_DH

    def regroup(t):
        parts = []
        for kh in range(SWA_KV_HEADS):
            khead = t[..., nq + kh * SWA_DH:nq + (kh + 1) * SWA_DH]
            vhead = t[..., nq + nk + kh * SWA_DH:nq + nk + (kh + 1) * SWA_DH]
            parts += [t[..., kh * group:(kh + 1) * group], khead, khead, vhead, vhead]
        return jnp.concatenate(parts, axis=-1)

    return regroup(w_qkv).astype(BF16), regroup(b_qkv)[None, :]


def _swa_ffn(x2d, cos2d, sin2d, seq, gain, wqkv, bqkv, sinks, w_out, b_out, fgain, wgu, wd, layer):
    tm = TM_SWA
    n_tiles = x2d.shape[0] // tm
    lane = np.arange(LANES)
    picks = (np.arange(LANES)[:, None] == (RET_QK // SWA_DH) * (lane % (SWA_DH // 2))[None, :])
    sign = np.where(lane % SWA_DH < SWA_DH // 2, -1.0, 1.0)
    sel = jnp.asarray(np.stack([picks * 1.0, picks * sign[None, :]]), F32)
    table = pl.BlockSpec((tm, LANES), lambda i: (jnp.minimum(i, n_tiles - 1), 0))
    tile_in, tile_out = _lagged_specs(n_tiles, tm)
    return pl.pallas_call(
        functools.partial(_swa_kernel, steps_per_seq=seq // tm),
        grid=(n_tiles + 1,),
        in_specs=[
            pl.BlockSpec(memory_space=pltpu.SMEM),
            tile_in,
            table,
            table,
            _layer_spec(gain, layer),
            _const_spec(wqkv.shape),
            _const_spec(bqkv.shape),
            _const_spec(sel.shape),
            _const_spec(w_out.shape),
            _const_spec((1, D_MODEL)),
            _layer_spec(fgain, layer),
            _layer_spec(wgu, layer),
            _layer_spec(wd, layer),
        ],
        out_specs=tile_out,
        out_shape=jax.ShapeDtypeStruct(x2d.shape, F32),
        scratch_shapes=[
            pltpu.VMEM((tm, SWA_Q_HEADS * SWA_DH), BF16),
            pltpu.VMEM((SWA_BLOCK + tm, SWA_KV_HEADS * LANES), BF16),
            pltpu.VMEM((SWA_BLOCK + tm, SWA_KV_HEADS * LANES), BF16),
            pltpu.VMEM((tm, SWA_Q_HEADS * SWA_DH), BF16),
            pltpu.VMEM((2, 2 * SWA_BLOCK, 4 * SWA_BLOCK), F32),
            pltpu.VMEM((tm, D_MODEL), F32),
            pltpu.VMEM((tm, FFN_H), BF16),
        ],
        compiler_params=_params(1),
        name="swa_ffn",
    )(sinks, x2d, cos2d, sin2d, gain, wqkv, bqkv, sel, w_out, b_out, fgain, wgu, wd)


def kernel(x, positions, mix_norm, ffn_norm, final_norm, ret_w_in, ret_w_out, lru_w_in, lru_conv_w, lru_conv_b, lru_w_r, lru_b_r, lru_w_i, lru_b_i, lru_lambda, lru_w_out, swa_w_qkv, swa_b_qkv, swa_sinks, swa_w_out, swa_b_out, ffn_w_gu, ffn_w_down):
    b, s, d = x.shape
    pos3 = positions.reshape(b, s, 1)
    fin = final_norm[None, :]
    mix_gain = mix_norm[:, None, :]
    ffn_gain = ffn_norm[:, None, :]
    ret_in, ret_out = ret_w_in.astype(BF16), ret_w_out.astype(BF16)
    ffn_gu, ffn_down = ffn_w_gu.astype(BF16), ffn_w_down.astype(BF16)
    rope = None
    for layer in range(DEPTH):
        kind = layer % N_MIXERS
        j = layer // N_MIXERS
        if kind == 0:
            if rope is None:
                x, cos, sin = _retention(x, pos3, mix_gain, ret_in, ret_out, layer, j)
                rope = (cos, sin)
            else:
                x = _retention(x, rope, mix_gain, ret_in, ret_out, layer, j)
            x = _ffn(x.reshape(b * s, d), ffn_gain, ffn_gu, ffn_down, fin, layer,
                     layer == DEPTH - 1).reshape(b, s, d)
        elif kind == 1:
            wri, bri = _lru_gate_weights(lru_w_r[j], lru_b_r[j], lru_w_i[j], lru_b_i[j])
            x = _rglru_ffn(x.reshape(b * s, d), s, mix_gain, _lru_in_weights(lru_w_in[j]),
                           lru_conv_w[j], lru_conv_b[j][None, :], wri, bri, lru_lambda[j][None, :],
                           lru_w_out[j].astype(BF16), ffn_gain, ffn_gu, ffn_down,
                           layer).reshape(b, s, d)
        else:
            wqkv, bqkv = _swa_weights(swa_w_qkv[j], swa_b_qkv[j])
            x = _swa_ffn(x.reshape(b * s, d), rope[0].reshape(b * s, LANES),
                         rope[1].reshape(b * s, LANES), s, mix_gain, wqkv, bqkv, swa_sinks[j],
                         swa_w_out[j].astype(BF16), swa_b_out[j][None, :], ffn_gain, ffn_gu,
                         ffn_down, layer).reshape(b, s, d)
    return x
```

```python
import functools

import jax
import jax.numpy as jnp
import numpy as np
from jax import lax
from jax.experimental import pallas as pl
from jax.experimental.pallas import tpu as pltpu

F32 = jnp.float32
BF16 = jnp.bfloat16

D_MODEL = 1024
DEPTH = 4
N_MIXERS = 3
NORM_EPS = 1e-6
ROPE_THETA = 10000.0

LANES = 128
SUBLANES = 8
BF16_ROWS = 2 * SUBLANES
STAGE_BYTES = 3 * 2 ** 19

RET_HEADS = 4
RET_QK = 256
RET_V = 512
RET_CHUNK = 256

LRU_W = 1536
LRU_BLOCKS = 8
LRU_BLOCK_W = 192
LRU_PAIR_W = 2 * LRU_BLOCK_W
LRU_PAIRS = LRU_BLOCKS // 2
LRU_C = 8.0
CONV_W = 4
LRU_GROUP = 256
LRU_NI = LRU_GROUP // SUBLANES
LRU_TAIL = (CONV_W - 1) * SUBLANES

SWA_DH = 64
SWA_Q_HEADS = 16
SWA_KV_HEADS = 4
SWA_WINDOW = 128
SWA_BLOCK = 128
SWA_NEG = -1e30

FFN_H = 2816
FFN_CHUNK = 256

VMEM_LIMIT = 56 * 1024 * 1024

TM_FFN = 1024
TM_RET = 512
TM_LRU = 512
TM_SWA = 512


def _rms(x, gain):
    return x * lax.rsqrt(jnp.mean(x * x, axis=-1, keepdims=True) + NORM_EPS) * gain


def _const_spec(shape):
    zeros = (0,) * len(shape)
    return pl.BlockSpec(shape, lambda *_: zeros, pipeline_mode=pl.Buffered(1))


def _layer_spec(stacked, layer):
    zeros = (0,) * (stacked.ndim - 1)
    return pl.BlockSpec((None,) + stacked.shape[1:], lambda *_: (layer,) + zeros,
                        pipeline_mode=pl.Buffered(1))


def _params(n_axes):
    return pltpu.CompilerParams(
        dimension_semantics=("arbitrary",) * n_axes, vmem_limit_bytes=VMEM_LIMIT)


class _Ffn:
    def __init__(self, x_ref, gain_ref, wgu_ref, wd_ref, act_ref):
        self.x = x_ref[...]
        self.h = _rms(self.x, gain_ref[...]).astype(BF16)
        self.wgu_ref, self.wd_ref, self.act_ref = wgu_ref, wd_ref, act_ref
        self.done = 0

    def advance(self, n):
        for _ in range(n):
            if self.done == FFN_H // FFN_CHUNK:
                return
            lo = self.done * FFN_CHUNK
            g = jnp.dot(self.h, self.wgu_ref[:, lo:lo + FFN_CHUNK], preferred_element_type=F32)
            u = jnp.dot(self.h, self.wgu_ref[:, FFN_H + lo:FFN_H + lo + FFN_CHUNK],
                        preferred_element_type=F32)
            self.act_ref[:, lo:lo + FFN_CHUNK] = (g * jax.nn.sigmoid(g) * u).astype(BF16)
            self.done += 1

    def finish(self):
        self.advance(FFN_H // FFN_CHUNK)
        return self.x + jnp.dot(self.act_ref[...], self.wd_ref[...], preferred_element_type=F32)


def _lagged_steps(mixer, make_ffn, o_ref, xmid_s):
    @pl.when(pl.program_id(0) == 0)
    def _():
        xmid_s[...] = jnp.zeros_like(xmid_s)

    out, xmid = mixer(make_ffn())
    o_ref[...] = out
    xmid_s[...] = xmid


def _stage_rows(k, n):
    best = BF16_ROWS
    for rows in range(BF16_ROWS, k + 1, BF16_ROWS):
        if k % rows == 0 and rows * n * 4 <= STAGE_BYTES:
            best = rows
    return best


def _resident_scratch(stacked):
    _, k, n = stacked.shape
    return [pltpu.VMEM((k, n), BF16), pltpu.VMEM((2, _stage_rows(k, n), n), F32),
            pltpu.SemaphoreType.DMA((2,))]


def _load_bf16(src, dst, stage, sem):
    rows = stage.shape[1]
    n_chunks = dst.shape[0] // rows

    def copy(c, slot):
        return pltpu.make_async_copy(src.at[pl.ds(c * rows, rows), :], stage.at[slot], sem.at[slot])

    copy(0, 0).start()

    def body(c, carry):
        slot = c % 2

        @pl.when(c + 1 < n_chunks)
        def _():
            copy(c + 1, 1 - slot).start()

        copy(c, slot).wait()
        dst[pl.ds(pl.multiple_of(c * rows, rows), rows), :] = stage[slot].astype(BF16)
        return carry

    lax.fori_loop(0, n_chunks, body, 0)


def _ffn_kernel(x_ref, gain_ref, wgu_hbm, wd_hbm, fin_ref, o_ref, act_ref,
                wgu_v, wgu_stage, wgu_sem, wd_v, wd_stage, wd_sem, *, layer, final):
    @pl.when(pl.program_id(0) == 0)
    def _():
        _load_bf16(wgu_hbm.at[layer], wgu_v, wgu_stage, wgu_sem)
        _load_bf16(wd_hbm.at[layer], wd_v, wd_stage, wd_sem)

    y = _Ffn(x_ref, gain_ref, wgu_v, wd_v, act_ref).finish()
    if final:
        y = _rms(y, fin_ref[...])
    o_ref[...] = y


def _ffn(x2d, gain, wgu, wd, fin, layer, final):
    t = x2d.shape[0]
    tm = TM_FFN
    return pl.pallas_call(
        functools.partial(_ffn_kernel, layer=layer, final=final),
        grid=(t // tm,),
        in_specs=[
            pl.BlockSpec((tm, D_MODEL), lambda i: (i, 0)),
            _layer_spec(gain, layer),
            pl.BlockSpec(memory_space=pl.ANY),
            pl.BlockSpec(memory_space=pl.ANY),
            _const_spec((1, D_MODEL)),
        ],
        out_specs=pl.BlockSpec((tm, D_MODEL), lambda i: (i, 0)),
        out_shape=jax.ShapeDtypeStruct(x2d.shape, F32),
        scratch_shapes=[pltpu.VMEM((tm, FFN_H), BF16)] + _resident_scratch(wgu)
        + _resident_scratch(wd),
        compiler_params=_params(1),
        name="ffn",
    )(x2d, gain, wgu, wd, fin)


def _ret_kernel(*refs, chunk_decay, emit_tables, j):
    if emit_tables:
        (x_ref, pos_ref, invf_ref, gain_ref, win_hbm, wout_hbm, intra_ref, qdec_ref, kdec_ref,
         o_ref, cos_ref, sin_ref, *scratch) = refs
    else:
        (x_ref, cos_ref, sin_ref, gain_ref, win_hbm, wout_hbm, intra_ref, qdec_ref, kdec_ref,
         o_ref, *scratch) = refs
    og_s, state_s, win_ref, win_stage, win_sem, wout_ref, wout_stage, wout_sem = scratch
    tm = x_ref.shape[0]
    half = RET_QK // 2

    @pl.when((pl.program_id(0) == 0) & (pl.program_id(1) == 0))
    def _():
        _load_bf16(win_hbm.at[j], win_ref, win_stage, win_sem)
        _load_bf16(wout_hbm.at[j], wout_ref, wout_stage, wout_sem)

    @pl.when(pl.program_id(1) == 0)
    def _():
        state_s[...] = jnp.zeros_like(state_s)

    x = x_ref[...]
    h = _rms(x, gain_ref[...]).astype(BF16)
    scale = RET_QK ** -0.5
    hq, hv = RET_HEADS * RET_QK, RET_HEADS * RET_V

    def project(hd):
        return [jnp.dot(h, win_ref[:, lo:lo + width], preferred_element_type=F32)
                for lo, width in ((hd * RET_QK, RET_QK), (hq + hd * RET_QK, RET_QK),
                                  (2 * hq + hd * RET_V, RET_V), (2 * hq + hv + hd * RET_V, RET_V))]

    if emit_tables:
        ang = pos_ref[...].astype(F32) * invf_ref[...]
        cos = jnp.cos(ang)
        sin = jnp.sin(ang)
        cos_ref[...] = cos
        sin_ref[...] = sin
    else:
        cos = cos_ref[...]
        sin = sin_ref[...]
    proj_next = project(0)
    for hd in range(RET_HEADS):
        q, k, v, g = proj_next
        if hd + 1 < RET_HEADS:
            proj_next = project(hd + 1)
        q1, q2 = q[:, 0:half], q[:, half:RET_QK]
        k1, k2 = k[:, 0:half], k[:, half:RET_QK]
        q_rot = jnp.concatenate([q1 * cos - q2 * sin, q2 * cos + q1 * sin], axis=1).astype(BF16)
        k_rot = jnp.concatenate([k1 * cos - k2 * sin, k2 * cos + k1 * sin], axis=1) * scale
        k_dec = (k_rot * kdec_ref[hd]).astype(BF16)
        k_rot = k_rot.astype(BF16)
        v = v.astype(BF16)
        chunks = range(tm // RET_CHUNK)
        rows = [slice(c * RET_CHUNK, (c + 1) * RET_CHUNK) for c in chunks]
        scores = [lax.dot_general(q_rot[rows[c]], k_rot[rows[c]], (((1,), (1,)), ((), ())),
                                  preferred_element_type=F32) * intra_ref[hd] for c in chunks]
        inner = [jnp.dot(scores[c].astype(BF16), v[rows[c]], preferred_element_type=F32)
                 for c in chunks]
        incr = [lax.dot_general(k_dec[rows[c]], v[rows[c]], (((0,), (0,)), ((), ())),
                                preferred_element_type=F32) for c in chunks]
        state = state_s[hd]
        gated = []
        for c in chunks:
            o = inner[c] + jnp.dot(q_rot[rows[c]], state.astype(BF16),
                                   preferred_element_type=F32) * qdec_ref[hd]
            state = state * chunk_decay[hd] + incr[c]
            o = o * lax.rsqrt(jnp.mean(o * o, axis=-1, keepdims=True) + NORM_EPS)
            gi = g[rows[c]]
            gated.append((o * (gi * jax.nn.sigmoid(gi))).astype(BF16))
        state_s[hd] = state
        og_s[:, hd * RET_V:(hd + 1) * RET_V] = jnp.concatenate(gated, axis=0)

    o_ref[...] = x + jnp.dot(og_s[...], wout_ref[...], preferred_element_type=F32)


def _ret_tables(tm):
    h, c = RET_HEADS, RET_CHUNK
    log_gamma = np.log1p(-(2.0 ** (-5.0 - np.arange(h, dtype=np.float64))))
    idx = np.arange(c, dtype=np.float64)
    rel = idx[:, None] - idx[None, :]
    intra = np.where(rel >= 0, np.exp(np.maximum(rel, 0.0) * log_gamma[:, None, None]), 0.0)
    q_decay = np.exp((idx + 1.0)[None, :] * log_gamma[:, None])[:, :, None]
    k_decay = np.exp((c - 1.0 - idx)[None, :] * log_gamma[:, None])
    k_decay = np.tile(k_decay, (1, tm // c))[:, :, None]
    chunk_decay = tuple(float(np.exp(c * lg)) for lg in log_gamma)
    half = RET_QK // 2
    inv_freq = (ROPE_THETA ** (-jnp.arange(half, dtype=F32) * 2.0 / RET_QK))[None, :]
    return (inv_freq, jnp.asarray(intra, F32), jnp.asarray(q_decay, F32),
            jnp.asarray(k_decay, F32), chunk_decay)


def _retention(x, rope, gain, w_in, w_out, layer, j):
    b, s, _ = x.shape
    tm = TM_RET
    inv_freq, intra, q_decay, k_decay, chunk_decay = _ret_tables(tm)
    emit_tables = not isinstance(rope, tuple)
    tok = pl.BlockSpec((None, tm, D_MODEL), lambda i, t: (i, t, 0))
    table = pl.BlockSpec((None, tm, LANES), lambda i, t: (i, t, 0))
    table_shape = jax.ShapeDtypeStruct((b, s, LANES), F32)
    if emit_tables:
        rope_args = (rope, inv_freq)
        rope_specs = [pl.BlockSpec((None, tm, 1), lambda i, t: (i, t, 0)),
                      _const_spec(inv_freq.shape)]
        out_specs, out_shape = [tok, table, table], [jax.ShapeDtypeStruct(x.shape, F32),
                                                     table_shape, table_shape]
    else:
        rope_args, rope_specs = rope, [table, table]
        out_specs, out_shape = tok, jax.ShapeDtypeStruct(x.shape, F32)
    return pl.pallas_call(
        functools.partial(_ret_kernel, chunk_decay=chunk_decay, emit_tables=emit_tables, j=j),
        grid=(b, s // tm),
        in_specs=[
            tok,
            *rope_specs,
            _layer_spec(gain, layer),
            pl.BlockSpec(memory_space=pl.ANY),
            pl.BlockSpec(memory_space=pl.ANY),
            _const_spec(intra.shape),
            _const_spec(q_decay.shape),
            _const_spec(k_decay.shape),
        ],
        out_specs=out_specs,
        out_shape=out_shape,
        scratch_shapes=[
            pltpu.VMEM((tm, RET_HEADS * RET_V), BF16),
            pltpu.VMEM((RET_HEADS, RET_QK, RET_V), F32),
        ] + _resident_scratch(w_in) + _resident_scratch(w_out),
        compiler_params=_params(2),
        name="retention",
    )(x, *rope_args, gain, w_in, w_out, intra, q_decay, k_decay)


def _lru_kernel(x_ref, gain_ref, perm_ref, permt_ref, win_ref, cw_ref, cb_ref, wri_ref, bri_ref,
                lam_ref, wout_ref, fgain_ref, wgu_hbm, wd_hbm, o_ref,
                hp_s, a_s, b_s, y_s, tail_s, hprev, xmid_s, act_s,
                wgu_ref, wgu_stage, wgu_sem, wd_ref, wd_stage, wd_sem, *, steps_per_seq, layer):
    @pl.when(pl.program_id(0) == 0)
    def _():
        _load_bf16(wgu_hbm.at[layer], wgu_ref, wgu_stage, wgu_sem)
        _load_bf16(wd_hbm.at[layer], wd_ref, wd_stage, wd_sem)

    @pl.when(pl.program_id(0) % steps_per_seq == 0)
    def _():
        tail_s[...] = jnp.zeros_like(tail_s)
        hprev[...] = jnp.zeros_like(hprev)

    mixer = functools.partial(
        _lru_mixer, x_ref=x_ref, gain_ref=gain_ref, perm_ref=perm_ref, permt_ref=permt_ref,
        win_ref=win_ref, cw_ref=cw_ref, cb_ref=cb_ref, wri_ref=wri_ref, bri_ref=bri_ref,
        lam_ref=lam_ref, wout_ref=wout_ref, hp_s=hp_s, a_s=a_s, b_s=b_s, y_s=y_s, tail_s=tail_s,
        hprev=hprev)
    make_ffn = functools.partial(_Ffn, xmid_s, fgain_ref, wgu_ref, wd_ref, act_s)
    _lagged_steps(mixer, make_ffn, o_ref, xmid_s)


def _lru_mixer(ffn, *, x_ref, gain_ref, perm_ref, permt_ref, win_ref, cw_ref, cb_ref, wri_ref,
               bri_ref, lam_ref, wout_ref, hp_s, a_s, b_s, y_s, tail_s, hprev):
    tm = x_ref.shape[0]
    G, NI, S8 = LRU_GROUP, LRU_NI, SUBLANES
    ngroups = tm // G
    x = x_ref[...]
    h = _rms(x, gain_ref[...]).astype(BF16)
    ffn.advance(2)
    ffn_out = None
    for g in range(ngroups):
        rows = slice(g * G, (g + 1) * G)
        hp_s[rows, :] = jnp.dot(perm_ref[...], h[rows, :], preferred_element_type=F32).astype(BF16)
    hp = hp_s[...]
    sub = lax.broadcasted_iota(jnp.int32, (S8, 1), 0)
    neg = -lam_ref[...]
    softplus = jnp.maximum(neg, 0.0) + jnp.log1p(jnp.exp(-jnp.abs(neg)))
    coef = -LRU_C * softplus

    def project(p):
        return jnp.dot(hp, win_ref[:, 2 * p * LRU_PAIR_W:2 * (p + 1) * LRU_PAIR_W],
                       preferred_element_type=F32)

    proj_next = project(0)
    for p in range(LRU_PAIRS):
        cols = slice(p * LRU_PAIR_W, (p + 1) * LRU_PAIR_W)
        proj = proj_next
        if p + 1 < LRU_PAIRS:
            proj_next = project(p + 1)
        if p + 1 < LRU_PAIRS:
            ffn.advance((2, 3, 3)[p])
        else:
            ffn_out = ffn.finish()
        gate = proj[:, 0:LRU_PAIR_W]
        u = proj[:, LRU_PAIR_W:2 * LRU_PAIR_W]

        conv = []
        for g in range(ngroups):
            u_g = u[g * G:(g + 1) * G, :]
            tail_prev = tail_s[:, cols]
            tail_cur = u_g[G - LRU_TAIL:G, :]
            heads = []
            for k in range(CONV_W - 1):
                sl = slice(k * S8, (k + 1) * S8)
                heads.append(jnp.where(sub == 0, pltpu.roll(tail_prev[sl, :], 1, 0),
                                       pltpu.roll(tail_cur[sl, :], 1, 0)))
            tail_s[:, cols] = tail_cur
            u_ext = jnp.concatenate(heads + [u_g], axis=0)
            uc_g = cb_ref[:, cols] + cw_ref[0:1, cols] * u_ext[0:G, :]
            for j in range(1, CONV_W):
                uc_g = uc_g + cw_ref[j:j + 1, cols] * u_ext[j * S8:j * S8 + G, :]
            conv.append(uc_g)
        uc = jnp.concatenate(conv, axis=0)

        ri = jnp.dot(uc.astype(BF16), wri_ref[p], preferred_element_type=F32) + bri_ref[p]
        r = 0.5 * jnp.tanh(0.5 * ri[:, 0:LRU_PAIR_W]) + 0.5
        i = 0.5 * jnp.tanh(0.5 * ri[:, LRU_PAIR_W:2 * LRU_PAIR_W]) + 0.5
        log_a = coef[:, cols] * r
        a_s[:, cols] = jnp.exp(log_a)
        th = jnp.tanh(log_a)
        num = -2.0 * th
        root = jnp.where(num == 0.0, 0.0, num * lax.rsqrt(num * (1.0 - th)))
        b_s[:, cols] = root * (i * uc)

        h0 = hprev[:, cols]
        accs_a = [a_s[g * G:g * G + S8, cols] for g in range(ngroups)]
        accs_b = [b_s[g * G:g * G + S8, cols] for g in range(ngroups)]
        for i_row in range(1, NI):
            for g in range(ngroups):
                r_sl = slice(g * G + i_row * S8, g * G + (i_row + 1) * S8)
                a_i = a_s[r_sl, cols]
                accs_b[g] = a_i * accs_b[g] + b_s[r_sl, cols]
                accs_a[g] = a_i * accs_a[g]
                a_s[r_sl, cols] = accs_a[g]
                b_s[r_sl, cols] = accs_b[g]
        for g in range(ngroups):
            base = g * G
            acc_a, acc_b = accs_a[g], accs_b[g]
            d = 1
            while d < S8:
                keep = sub >= d
                prev_a = jnp.where(keep, pltpu.roll(acc_a, d, 0), 1.0)
                prev_b = jnp.where(keep, pltpu.roll(acc_b, d, 0), 0.0)
                acc_b = acc_a * prev_b + acc_b
                acc_a = acc_a * prev_a
                d *= 2
            h_end = acc_a * h0 + acc_b
            carry = jnp.where(sub == 0, h0, pltpu.roll(h_end, 1, 0))
            rows = slice(base, base + G)
            hs = (b_s[rows, cols].reshape(NI, S8, LRU_PAIR_W)
                  + a_s[rows, cols].reshape(NI, S8, LRU_PAIR_W) * carry[None]
                  ).reshape(G, LRU_PAIR_W)
            y = (hs * jax.nn.gelu(gate[rows, :])).astype(BF16)
            y_s[rows, cols] = jnp.dot(permt_ref[...], y, preferred_element_type=F32).astype(BF16)
            h0 = h_end[S8 - 1:S8, :]
        hprev[:, cols] = h0

    return ffn_out, x + jnp.dot(y_s[...], wout_ref[...], preferred_element_type=F32)


def _lru_perm():
    p = np.arange(LRU_GROUP)
    time_of_row = (p % SUBLANES) * LRU_NI + p // SUBLANES
    perm = (time_of_row[:, None] == np.arange(LRU_GROUP)[None, :]).astype(np.float32)
    return jnp.asarray(perm, BF16), jnp.asarray(perm.T, BF16)


def _lru_in_weights(w_in):
    parts = []
    for p in range(LRU_PAIRS):
        lo, hi = p * LRU_PAIR_W, (p + 1) * LRU_PAIR_W
        parts += [w_in[:, lo:hi], w_in[:, LRU_W + lo:LRU_W + hi]]
    return jnp.concatenate(parts, axis=1).astype(BF16)


def _lru_gate_weights(w_r, b_r, w_i, b_i):
    bw = LRU_BLOCK_W
    z = jnp.zeros((bw, bw), w_r.dtype)
    mats, biases = [], []
    for p in range(LRU_PAIRS):
        n0, n1 = 2 * p, 2 * p + 1
        top = jnp.concatenate([w_r[n0], z, w_i[n0], z], axis=1)
        bot = jnp.concatenate([z, w_r[n1], z, w_i[n1]], axis=1)
        mats.append(jnp.concatenate([top, bot], axis=0))
        biases.append(jnp.concatenate([b_r[n0], b_r[n1], b_i[n0], b_i[n1]])[None, :])
    return jnp.stack(mats).astype(BF16), jnp.stack(biases)


def _lagged_specs(n_tiles, tm):
    tile_in = pl.BlockSpec((tm, D_MODEL), lambda i: (jnp.minimum(i, n_tiles - 1), 0))
    tile_out = pl.BlockSpec((tm, D_MODEL), lambda i: (jnp.maximum(i - 1, 0), 0))
    return tile_in, tile_out


def _rglru_ffn(x2d, seq, gain, w_in, conv_w, conv_b, wri, bri, lam, w_out, fgain, wgu, wd, layer):
    tm = TM_LRU
    n_tiles = x2d.shape[0] // tm
    perm, permt = _lru_perm()
    tile_in, tile_out = _lagged_specs(n_tiles, tm)
    return pl.pallas_call(
        functools.partial(_lru_kernel, steps_per_seq=seq // tm, layer=layer),
        grid=(n_tiles + 1,),
        in_specs=[
            tile_in,
            _layer_spec(gain, layer),
            _const_spec(perm.shape),
            _const_spec(permt.shape),
            _const_spec(w_in.shape),
            _const_spec(conv_w.shape),
            _const_spec(conv_b.shape),
            _const_spec(wri.shape),
            _const_spec(bri.shape),
            _const_spec(lam.shape),
            _const_spec(w_out.shape),
            _layer_spec(fgain, layer),
            pl.BlockSpec(memory_space=pl.ANY),
            pl.BlockSpec(memory_space=pl.ANY),
        ],
        out_specs=tile_out,
        out_shape=jax.ShapeDtypeStruct(x2d.shape, F32),
        scratch_shapes=[
            pltpu.VMEM((tm, D_MODEL), BF16),
            pltpu.VMEM((tm, LRU_W), F32),
            pltpu.VMEM((tm, LRU_W), F32),
            pltpu.VMEM((tm, LRU_W), BF16),
            pltpu.VMEM((LRU_TAIL, LRU_W), F32),
            pltpu.VMEM((1, LRU_W), F32),
            pltpu.VMEM((tm, D_MODEL), F32),
            pltpu.VMEM((tm, FFN_H), BF16),
        ] + _resident_scratch(wgu) + _resident_scratch(wd),
        compiler_params=_params(1),
        name="rglru_ffn",
    )(x2d, gain, perm, permt, w_in, conv_w, conv_b, wri, bri, lam, w_out, fgain, wgu, wd)


def _swa_rope(t, cos, sin_signed, low_half):
    partner = jnp.where(low_half, pltpu.roll(t, LANES - SWA_DH // 2, 1),
                        pltpu.roll(t, SWA_DH // 2, 1))
    return t * cos + partner * sin_signed


def _swa_kernel(sinks_ref, x_ref, cos_ref, sin_ref, gain_ref, wqkv_ref, bqkv_ref, sel_ref,
                wout_ref, bout_ref, fgain_ref, wgu_hbm, wd_hbm, o_ref,
                q_s, kbuf, vbuf, att_s, bias_s, xmid_s, act_s,
                wgu_ref, wgu_stage, wgu_sem, wd_ref, wd_stage, wd_sem, *, steps_per_seq, layer):
    @pl.when(pl.program_id(0) == 0)
    def _():
        _load_bf16(wgu_hbm.at[layer], wgu_ref, wgu_stage, wgu_sem)
        _load_bf16(wd_hbm.at[layer], wd_ref, wd_stage, wd_sem)

    seq_start = pl.program_id(0) % steps_per_seq == 0

    @pl.when(seq_start)
    def _():
        halo = (SWA_BLOCK, SWA_KV_HEADS * LANES)
        kbuf[0:SWA_BLOCK, :] = jnp.zeros(halo, BF16)
        vbuf[0:SWA_BLOCK, :] = jnp.zeros(halo, BF16)

    mixer = functools.partial(
        _swa_mixer, seq_start=seq_start, sinks_ref=sinks_ref, x_ref=x_ref, cos_ref=cos_ref,
        sin_ref=sin_ref, gain_ref=gain_ref, wqkv_ref=wqkv_ref, bqkv_ref=bqkv_ref, sel_ref=sel_ref,
        wout_ref=wout_ref, bout_ref=bout_ref, q_s=q_s, kbuf=kbuf, vbuf=vbuf, att_s=att_s,
        bias_s=bias_s)
    make_ffn = functools.partial(_Ffn, xmid_s, fgain_ref, wgu_ref, wd_ref, act_s)
    _lagged_steps(mixer, make_ffn, o_ref, xmid_s)


def _swa_mixer(ffn, *, seq_start, sinks_ref, x_ref, cos_ref, sin_ref, gain_ref, wqkv_ref, bqkv_ref,
               sel_ref, wout_ref, bout_ref, q_s, kbuf, vbuf, att_s, bias_s):
    tm = x_ref.shape[0]
    L = SWA_BLOCK
    x = x_ref[...]
    h = _rms(x, gain_ref[...]).astype(BF16)
    grp = 4 * LANES

    def project(kh):
        return (jnp.dot(h, wqkv_ref[:, kh * grp:(kh + 1) * grp], preferred_element_type=F32)
                + bqkv_ref[:, kh * grp:(kh + 1) * grp])

    proj_next = project(0)
    ffn.advance(3)
    ffn_out = None

    qi = lax.broadcasted_iota(jnp.int32, (2 * L, 4 * L), 0) % L
    si = lax.broadcasted_iota(jnp.int32, (2 * L, 4 * L), 1) % (2 * L)
    dist = qi + L - si
    in_window = (dist >= 0) & (dist < SWA_WINDOW)
    has_prev = (si >= L) | jnp.logical_not(seq_start)
    bias_s[0] = jnp.where(in_window & has_prev, 0.0, SWA_NEG)
    bias_s[1] = jnp.where(in_window, 0.0, SWA_NEG)

    lane = lax.broadcasted_iota(jnp.int32, (1, LANES), 1)
    low_half = (lane % SWA_DH) < (SWA_DH // 2)
    cos = jnp.dot(cos_ref[...], sel_ref[0], precision=lax.Precision.HIGHEST,
                  preferred_element_type=F32)
    sin_signed = jnp.dot(sin_ref[...], sel_ref[1], precision=lax.Precision.HIGHEST,
                         preferred_element_type=F32)

    first_head = lane < SWA_DH
    row_is_top = lax.broadcasted_iota(jnp.int32, (2 * L, 1), 0) < L
    mixed = x + bout_ref[...]

    for kh in range(SWA_KV_HEADS):
        base = kh * 2 * LANES
        kcols = slice(kh * LANES, (kh + 1) * LANES)
        proj = proj_next
        if kh + 1 < SWA_KV_HEADS:
            proj_next = project(kh + 1)
        if kh + 1 < SWA_KV_HEADS:
            ffn.advance((2, 3, 3)[kh])
        else:
            ffn_out = ffn.finish()
        for j in range(2):
            q_s[:, base + j * LANES:base + (j + 1) * LANES] = (
                _swa_rope(proj[:, j * LANES:(j + 1) * LANES], cos, sin_signed, low_half)
                * (SWA_DH ** -0.5)).astype(BF16)
        kbuf[L:L + tm, kcols] = _swa_rope(
            proj[:, 2 * LANES:3 * LANES], cos, sin_signed, low_half).astype(BF16)
        vbuf[L:L + tm, kcols] = proj[:, 3 * LANES:4 * LANES].astype(BF16)

        blocks = range(tm // L)
        sinks = [jnp.where(row_is_top, sinks_ref[4 * kh + par], sinks_ref[4 * kh + 2 + par])
                 for par in range(2)]
        scores, vcats = [], []
        for c in blocks:
            rows = slice(c * L, (c + 1) * L)
            qst = jnp.concatenate(
                [q_s[rows, base:base + LANES], q_s[rows, base + LANES:base + 2 * LANES]], axis=0)
            kd = kbuf[c * L:c * L + 2 * L, kcols]
            vd = vbuf[c * L:c * L + 2 * L, kcols]
            zero = jnp.zeros_like(kd)
            kcat = jnp.concatenate([jnp.where(first_head, kd, zero),
                                    jnp.where(first_head, zero, kd)], axis=0)
            vcats.append(jnp.concatenate([jnp.where(first_head, vd, zero),
                                          jnp.where(first_head, zero, vd)], axis=0))
            sc = lax.dot_general(qst, kcat, (((1,), (1,)), ((), ())),
                                 preferred_element_type=F32) + bias_s[0 if c == 0 else 1]
            scores.append([sc[:, par * 2 * L:(par + 1) * 2 * L] for par in range(2)])
        maxes = [[jnp.maximum(jnp.max(scores[c][par], axis=-1, keepdims=True), sinks[par])
                  for par in range(2)] for c in blocks]
        expd = [[jnp.exp(scores[c][par] - maxes[c][par]) for par in range(2)] for c in blocks]
        dens = [[jnp.sum(expd[c][par], axis=-1, keepdims=True)
                 + jnp.exp(sinks[par] - maxes[c][par]) for par in range(2)] for c in blocks]
        for c in blocks:
            rows = slice(c * L, (c + 1) * L)
            probs = jnp.concatenate(
                [(expd[c][par] * (1.0 / dens[c][par])).astype(BF16) for par in range(2)], axis=1)
            out = jnp.dot(probs, vcats[c], preferred_element_type=F32)
            att_s[rows, base:base + LANES] = out[0:L].astype(BF16)
            att_s[rows, base + LANES:base + 2 * LANES] = out[L:2 * L].astype(BF16)
        mixed = mixed + jnp.dot(att_s[:, base:base + 2 * LANES], wout_ref[base:base + 2 * LANES, :],
                                preferred_element_type=F32)

    kbuf[0:L, :] = kbuf[tm:tm + L, :]
    vbuf[0:L, :] = vbuf[tm:tm + L, :]
    return ffn_out, mixed


def _swa_weights(w_qkv, b_qkv):
    nq = SWA_Q_HEADS * SWA_DH
    nk = SWA_KV_HEADS * SWA_DH
    group = SWA_Q_HEADS // SWA_KV_HEADS * SWA_DH

    def regroup(t):
        parts = []
        for kh in range(SWA_KV_HEADS):
            khead = t[..., nq + kh * SWA_DH:nq + (kh + 1) * SWA_DH]
            vhead = t[..., nq + nk + kh * SWA_DH:nq + nk + (kh + 1) * SWA_DH]
            parts += [t[..., kh * group:(kh + 1) * group], khead, khead, vhead, vhead]
        return jnp.concatenate(parts, axis=-1)

    return regroup(w_qkv).astype(BF16), regroup(b_qkv)[None, :]


def _swa_ffn(x2d, cos2d, sin2d, seq, gain, wqkv, bqkv, sinks, w_out, b_out, fgain, wgu, wd, layer):
    tm = TM_SWA
    n_tiles = x2d.shape[0] // tm
    lane = np.arange(LANES)
    picks = (np.arange(LANES)[:, None] == (RET_QK // SWA_DH) * (lane % (SWA_DH // 2))[None, :])
    sign = np.where(lane % SWA_DH < SWA_DH // 2, -1.0, 1.0)
    sel = jnp.asarray(np.stack([picks * 1.0, picks * sign[None, :]]), F32)
    table = pl.BlockSpec((tm, LANES), lambda i: (jnp.minimum(i, n_tiles - 1), 0))
    tile_in, tile_out = _lagged_specs(n_tiles, tm)
    return pl.pallas_call(
        functools.partial(_swa_kernel, steps_per_seq=seq // tm, layer=layer),
        grid=(n_tiles + 1,),
        in_specs=[
            pl.BlockSpec(memory_space=pltpu.SMEM),
            tile_in,
            table,
            table,
            _layer_spec(gain, layer),
            _const_spec(wqkv.shape),
            _const_spec(bqkv.shape),
            _const_spec(sel.shape),
            _const_spec(w_out.shape),
            _const_spec((1, D_MODEL)),
            _layer_spec(fgain, layer),
            pl.BlockSpec(memory_space=pl.ANY),
            pl.BlockSpec(memory_space=pl.ANY),
        ],
        out_specs=tile_out,
        out_shape=jax.ShapeDtypeStruct(x2d.shape, F32),
        scratch_shapes=[
            pltpu.VMEM((tm, SWA_Q_HEADS * SWA_DH), BF16),
            pltpu.VMEM((SWA_BLOCK + tm, SWA_KV_HEADS * LANES), BF16),
            pltpu.VMEM((SWA_BLOCK + tm, SWA_KV_HEADS * LANES), BF16),
            pltpu.VMEM((tm, SWA_Q_HEADS * SWA_DH), BF16),
            pltpu.VMEM((2, 2 * SWA_BLOCK, 4 * SWA_BLOCK), F32),
            pltpu.VMEM((tm, D_MODEL), F32),
            pltpu.VMEM((tm, FFN_H), BF16),
        ] + _resident_scratch(wgu) + _resident_scratch(wd),
        compiler_params=_params(1),
        name="swa_ffn",
    )(sinks, x2d, cos2d, sin2d, gain, wqkv, bqkv, sel, w_out, b_out, fgain, wgu, wd)


def kernel(x, positions, mix_norm, ffn_norm, final_norm, ret_w_in, ret_w_out, lru_w_in, lru_conv_w, lru_conv_b, lru_w_r, lru_b_r, lru_w_i, lru_b_i, lru_lambda, lru_w_out, swa_w_qkv, swa_b_qkv, swa_sinks, swa_w_out, swa_b_out, ffn_w_gu, ffn_w_down):
    b, s, d = x.shape
    pos3 = positions.reshape(b, s, 1)
    fin = final_norm[None, :]
    mix_gain = mix_norm[:, None, :]
    ffn_gain = ffn_norm[:, None, :]
    ret_in, ret_out, ffn_gu, ffn_down = ret_w_in, ret_w_out, ffn_w_gu, ffn_w_down
    rope = None
    for layer in range(DEPTH):
        kind = layer % N_MIXERS
        j = layer // N_MIXERS
        if kind == 0:
            if rope is None:
                x, cos, sin = _retention(x, pos3, mix_gain, ret_in, ret_out, layer, j)
                rope = (cos, sin)
            else:
                x = _retention(x, rope, mix_gain, ret_in, ret_out, layer, j)
            x = _ffn(x.reshape(b * s, d), ffn_gain, ffn_gu, ffn_down, fin, layer,
                     layer == DEPTH - 1).reshape(b, s, d)
        elif kind == 1:
            wri, bri = _lru_gate_weights(lru_w_r[j], lru_b_r[j], lru_w_i[j], lru_b_i[j])
            x = _rglru_ffn(x.reshape(b * s, d), s, mix_gain, _lru_in_weights(lru_w_in[j]),
                           lru_conv_w[j], lru_conv_b[j][None, :], wri, bri, lru_lambda[j][None, :],
                           lru_w_out[j].astype(BF16), ffn_gain, ffn_gu, ffn_down,
                           layer).reshape(b, s, d)
        else:
            wqkv, bqkv = _swa_weights(swa_w_qkv[j], swa_b_qkv[j])
            x = _swa_ffn(x.reshape(b * s, d), rope[0].reshape(b * s, LANES),
                         rope[1].reshape(b * s, LANES), s, mix_gain, wqkv, bqkv, swa_sinks[j],
                         swa_w_out[j].astype(BF16), swa_b_out[j][None, :], ffn_gain, ffn_gu,
                         ffn_down, layer).reshape(b, s, d)
    return x
```

```python
import functools

import jax
import jax.numpy as jnp
import numpy as np
from jax import lax
from jax.experimental import pallas as pl
from jax.experimental.pallas import tpu as pltpu

F32 = jnp.float32
BF16 = jnp.bfloat16

D_MODEL = 1024
DEPTH = 4
N_MIXERS = 3
NORM_EPS = 1e-6
ROPE_THETA = 10000.0

LANES = 128
SUBLANES = 8
BF16_ROWS = 2 * SUBLANES
STAGE_BYTES = 3 * 2 ** 18
STAGE_SLOTS = 4

RET_HEADS = 4
RET_QK = 256
RET_V = 512
RET_CHUNK = 256

LRU_W = 1536
LRU_BLOCKS = 8
LRU_BLOCK_W = 192
LRU_PAIR_W = 2 * LRU_BLOCK_W
LRU_PAIRS = LRU_BLOCKS // 2
LRU_C = 8.0
CONV_W = 4
LRU_GROUP = 256
LRU_NI = LRU_GROUP // SUBLANES
LRU_TAIL = (CONV_W - 1) * SUBLANES

SWA_DH = 64
SWA_Q_HEADS = 16
SWA_KV_HEADS = 4
SWA_WINDOW = 128
SWA_BLOCK = 128
SWA_NEG = -1e30

FFN_H = 2816
FFN_CHUNK = 256

VMEM_LIMIT = 56 * 1024 * 1024

TM_FFN = 1024
TM_RET = 512
TM_LRU = 512
TM_SWA = 512


def _rms(x, gain):
    return x * lax.rsqrt(jnp.mean(x * x, axis=-1, keepdims=True) + NORM_EPS) * gain


def _const_spec(shape):
    zeros = (0,) * len(shape)
    return pl.BlockSpec(shape, lambda *_: zeros, pipeline_mode=pl.Buffered(1))


def _layer_spec(stacked, layer):
    zeros = (0,) * (stacked.ndim - 1)
    return pl.BlockSpec((None,) + stacked.shape[1:], lambda *_: (layer,) + zeros,
                        pipeline_mode=pl.Buffered(1))


def _params(n_axes):
    return pltpu.CompilerParams(
        dimension_semantics=("arbitrary",) * n_axes, vmem_limit_bytes=VMEM_LIMIT)


class _Ffn:
    def __init__(self, x_ref, gain_ref, wgu_ref, wd_ref, act_ref):
        self.x = x_ref[...]
        self.h = _rms(self.x, gain_ref[...]).astype(BF16)
        self.wgu_ref, self.wd_ref, self.act_ref = wgu_ref, wd_ref, act_ref
        self.done = 0

    def advance(self, n):
        for _ in range(n):
            if self.done == FFN_H // FFN_CHUNK:
                return
            lo = self.done * FFN_CHUNK
            g = jnp.dot(self.h, self.wgu_ref[:, lo:lo + FFN_CHUNK], preferred_element_type=F32)
            u = jnp.dot(self.h, self.wgu_ref[:, FFN_H + lo:FFN_H + lo + FFN_CHUNK],
                        preferred_element_type=F32)
            self.act_ref[:, lo:lo + FFN_CHUNK] = (g * jax.nn.sigmoid(g) * u).astype(BF16)
            self.done += 1

    def finish(self):
        self.advance(FFN_H // FFN_CHUNK)
        return self.x + jnp.dot(self.act_ref[...], self.wd_ref[...], preferred_element_type=F32)


def _lagged_steps(mixer, make_ffn, o_ref, xmid_s):
    @pl.when(pl.program_id(0) == 0)
    def _():
        xmid_s[...] = jnp.zeros_like(xmid_s)

    out, xmid = mixer(make_ffn())
    o_ref[...] = out
    xmid_s[...] = xmid


def _stage_rows(k, n):
    best = BF16_ROWS
    for rows in range(BF16_ROWS, k + 1, BF16_ROWS):
        if k % rows == 0 and rows * n * 4 <= STAGE_BYTES:
            best = rows
    return best


def _resident_scratch(stacked):
    _, k, n = stacked.shape
    return [pltpu.VMEM((k, n), BF16), pltpu.VMEM((STAGE_SLOTS, _stage_rows(k, n), n), F32),
            pltpu.SemaphoreType.DMA((STAGE_SLOTS,))]


def _load_bf16(src, dst, stage, sem):
    slots, rows = stage.shape[0], stage.shape[1]
    n_chunks = dst.shape[0] // rows
    ahead = slots - 1

    def copy(c):
        slot = c % slots
        return pltpu.make_async_copy(src.at[pl.ds(c * rows, rows), :], stage.at[slot], sem.at[slot])

    for c in range(min(ahead, n_chunks)):
        copy(c).start()

    def body(c, carry):
        @pl.when(c + ahead < n_chunks)
        def _():
            copy(c + ahead).start()

        copy(c).wait()
        dst[pl.ds(pl.multiple_of(c * rows, rows), rows), :] = stage[c % slots].astype(BF16)
        return carry

    lax.fori_loop(0, n_chunks, body, 0)


def _ffn_kernel(x_ref, gain_ref, wgu_hbm, wd_hbm, fin_ref, o_ref, act_ref,
                wgu_v, wgu_stage, wgu_sem, wd_v, wd_stage, wd_sem, *, layer, final):
    @pl.when(pl.program_id(0) == 0)
    def _():
        _load_bf16(wgu_hbm.at[layer], wgu_v, wgu_stage, wgu_sem)
        _load_bf16(wd_hbm.at[layer], wd_v, wd_stage, wd_sem)

    y = _Ffn(x_ref, gain_ref, wgu_v, wd_v, act_ref).finish()
    if final:
        y = _rms(y, fin_ref[...])
    o_ref[...] = y


def _ffn(x2d, gain, wgu, wd, fin, layer, final):
    t = x2d.shape[0]
    tm = TM_FFN
    return pl.pallas_call(
        functools.partial(_ffn_kernel, layer=layer, final=final),
        grid=(t // tm,),
        in_specs=[
            pl.BlockSpec((tm, D_MODEL), lambda i: (i, 0)),
            _layer_spec(gain, layer),
            pl.BlockSpec(memory_space=pl.ANY),
            pl.BlockSpec(memory_space=pl.ANY),
            _const_spec((1, D_MODEL)),
        ],
        out_specs=pl.BlockSpec((tm, D_MODEL), lambda i: (i, 0)),
        out_shape=jax.ShapeDtypeStruct(x2d.shape, F32),
        scratch_shapes=[pltpu.VMEM((tm, FFN_H), BF16)] + _resident_scratch(wgu)
        + _resident_scratch(wd),
        compiler_params=_params(1),
        name="ffn",
    )(x2d, gain, wgu, wd, fin)


def _ret_kernel(*refs, chunk_decay, emit_tables, j):
    if emit_tables:
        (x_ref, pos_ref, invf_ref, gain_ref, win_hbm, wout_hbm, intra_ref, qdec_ref, kdec_ref,
         o_ref, cos_ref, sin_ref, *scratch) = refs
    else:
        (x_ref, cos_ref, sin_ref, gain_ref, win_hbm, wout_hbm, intra_ref, qdec_ref, kdec_ref,
         o_ref, *scratch) = refs
    og_s, state_s, win_ref, win_stage, win_sem, wout_ref, wout_stage, wout_sem = scratch
    tm = x_ref.shape[0]
    half = RET_QK // 2

    @pl.when((pl.program_id(0) == 0) & (pl.program_id(1) == 0))
    def _():
        _load_bf16(win_hbm.at[j], win_ref, win_stage, win_sem)
        _load_bf16(wout_hbm.at[j], wout_ref, wout_stage, wout_sem)

    @pl.when(pl.program_id(1) == 0)
    def _():
        state_s[...] = jnp.zeros_like(state_s)

    x = x_ref[...]
    h = _rms(x, gain_ref[...]).astype(BF16)
    scale = RET_QK ** -0.5
    hq, hv = RET_HEADS * RET_QK, RET_HEADS * RET_V

    def project(hd):
        return [jnp.dot(h, win_ref[:, lo:lo + width], preferred_element_type=F32)
                for lo, width in ((hd * RET_QK, RET_QK), (hq + hd * RET_QK, RET_QK),
                                  (2 * hq + hd * RET_V, RET_V), (2 * hq + hv + hd * RET_V, RET_V))]

    if emit_tables:
        ang = pos_ref[...].astype(F32) * invf_ref[...]
        cos = jnp.cos(ang)
        sin = jnp.sin(ang)
        cos_ref[...] = cos
        sin_ref[...] = sin
    else:
        cos = cos_ref[...]
        sin = sin_ref[...]
    proj_next = project(0)
    for hd in range(RET_HEADS):
        q, k, v, g = proj_next
        if hd + 1 < RET_HEADS:
            proj_next = project(hd + 1)
        q1, q2 = q[:, 0:half], q[:, half:RET_QK]
        k1, k2 = k[:, 0:half], k[:, half:RET_QK]
        q_rot = jnp.concatenate([q1 * cos - q2 * sin, q2 * cos + q1 * sin], axis=1).astype(BF16)
        k_rot = jnp.concatenate([k1 * cos - k2 * sin, k2 * cos + k1 * sin], axis=1) * scale
        k_dec = (k_rot * kdec_ref[hd]).astype(BF16)
        k_rot = k_rot.astype(BF16)
        v = v.astype(BF16)
        chunks = range(tm // RET_CHUNK)
        rows = [slice(c * RET_CHUNK, (c + 1) * RET_CHUNK) for c in chunks]
        scores = [lax.dot_general(q_rot[rows[c]], k_rot[rows[c]], (((1,), (1,)), ((), ())),
                                  preferred_element_type=F32) * intra_ref[hd] for c in chunks]
        inner = [jnp.dot(scores[c].astype(BF16), v[rows[c]], preferred_element_type=F32)
                 for c in chunks]
        incr = [lax.dot_general(k_dec[rows[c]], v[rows[c]], (((0,), (0,)), ((), ())),
                                preferred_element_type=F32) for c in chunks]
        state = state_s[hd]
        gated = []
        for c in chunks:
            o = inner[c] + jnp.dot(q_rot[rows[c]], state.astype(BF16),
                                   preferred_element_type=F32) * qdec_ref[hd]
            state = state * chunk_decay[hd] + incr[c]
            o = o * lax.rsqrt(jnp.mean(o * o, axis=-1, keepdims=True) + NORM_EPS)
            gi = g[rows[c]]
            gated.append((o * (gi * jax.nn.sigmoid(gi))).astype(BF16))
        state_s[hd] = state
        og_s[:, hd * RET_V:(hd + 1) * RET_V] = jnp.concatenate(gated, axis=0)

    o_ref[...] = x + jnp.dot(og_s[...], wout_ref[...], preferred_element_type=F32)


def _ret_tables(tm):
    h, c = RET_HEADS, RET_CHUNK
    log_gamma = np.log1p(-(2.0 ** (-5.0 - np.arange(h, dtype=np.float64))))
    idx = np.arange(c, dtype=np.float64)
    rel = idx[:, None] - idx[None, :]
    intra = np.where(rel >= 0, np.exp(np.maximum(rel, 0.0) * log_gamma[:, None, None]), 0.0)
    q_decay = np.exp((idx + 1.0)[None, :] * log_gamma[:, None])[:, :, None]
    k_decay = np.exp((c - 1.0 - idx)[None, :] * log_gamma[:, None])
    k_decay = np.tile(k_decay, (1, tm // c))[:, :, None]
    chunk_decay = tuple(float(np.exp(c * lg)) for lg in log_gamma)
    half = RET_QK // 2
    inv_freq = (ROPE_THETA ** (-jnp.arange(half, dtype=F32) * 2.0 / RET_QK))[None, :]
    return (inv_freq, jnp.asarray(intra, F32), jnp.asarray(q_decay, F32),
            jnp.asarray(k_decay, F32), chunk_decay)


def _retention(x, rope, gain, w_in, w_out, layer, j):
    b, s, _ = x.shape
    tm = TM_RET
    inv_freq, intra, q_decay, k_decay, chunk_decay = _ret_tables(tm)
    emit_tables = not isinstance(rope, tuple)
    tok = pl.BlockSpec((None, tm, D_MODEL), lambda i, t: (i, t, 0))
    table = pl.BlockSpec((None, tm, LANES), lambda i, t: (i, t, 0))
    table_shape = jax.ShapeDtypeStruct((b, s, LANES), F32)
    if emit_tables:
        rope_args = (rope, inv_freq)
        rope_specs = [pl.BlockSpec((None, tm, 1), lambda i, t: (i, t, 0)),
                      _const_spec(inv_freq.shape)]
        out_specs, out_shape = [tok, table, table], [jax.ShapeDtypeStruct(x.shape, F32),
                                                     table_shape, table_shape]
    else:
        rope_args, rope_specs = rope, [table, table]
        out_specs, out_shape = tok, jax.ShapeDtypeStruct(x.shape, F32)
    return pl.pallas_call(
        functools.partial(_ret_kernel, chunk_decay=chunk_decay, emit_tables=emit_tables, j=j),
        grid=(b, s // tm),
        in_specs=[
            tok,
            *rope_specs,
            _layer_spec(gain, layer),
            pl.BlockSpec(memory_space=pl.ANY),
            pl.BlockSpec(memory_space=pl.ANY),
            _const_spec(intra.shape),
            _const_spec(q_decay.shape),
            _const_spec(k_decay.shape),
        ],
        out_specs=out_specs,
        out_shape=out_shape,
        scratch_shapes=[
            pltpu.VMEM((tm, RET_HEADS * RET_V), BF16),
            pltpu.VMEM((RET_HEADS, RET_QK, RET_V), F32),
        ] + _resident_scratch(w_in) + _resident_scratch(w_out),
        compiler_params=_params(2),
        name="retention",
    )(x, *rope_args, gain, w_in, w_out, intra, q_decay, k_decay)


def _lru_kernel(x_ref, gain_ref, perm_ref, permt_ref, win_ref, cw_ref, cb_ref, wri_ref, bri_ref,
                lam_ref, wout_ref, fgain_ref, wgu_hbm, wd_hbm, o_ref,
                hp_s, a_s, b_s, y_s, tail_s, hprev, xmid_s, act_s,
                wgu_ref, wgu_stage, wgu_sem, wd_ref, wd_stage, wd_sem, *, steps_per_seq, layer):
    @pl.when(pl.program_id(0) == 0)
    def _():
        _load_bf16(wgu_hbm.at[layer], wgu_ref, wgu_stage, wgu_sem)
        _load_bf16(wd_hbm.at[layer], wd_ref, wd_stage, wd_sem)

    @pl.when(pl.program_id(0) % steps_per_seq == 0)
    def _():
        tail_s[...] = jnp.zeros_like(tail_s)
        hprev[...] = jnp.zeros_like(hprev)

    mixer = functools.partial(
        _lru_mixer, x_ref=x_ref, gain_ref=gain_ref, perm_ref=perm_ref, permt_ref=permt_ref,
        win_ref=win_ref, cw_ref=cw_ref, cb_ref=cb_ref, wri_ref=wri_ref, bri_ref=bri_ref,
        lam_ref=lam_ref, wout_ref=wout_ref, hp_s=hp_s, a_s=a_s, b_s=b_s, y_s=y_s, tail_s=tail_s,
        hprev=hprev)
    make_ffn = functools.partial(_Ffn, xmid_s, fgain_ref, wgu_ref, wd_ref, act_s)
    _lagged_steps(mixer, make_ffn, o_ref, xmid_s)


def _lru_mixer(ffn, *, x_ref, gain_ref, perm_ref, permt_ref, win_ref, cw_ref, cb_ref, wri_ref,
               bri_ref, lam_ref, wout_ref, hp_s, a_s, b_s, y_s, tail_s, hprev):
    tm = x_ref.shape[0]
    G, NI, S8 = LRU_GROUP, LRU_NI, SUBLANES
    ngroups = tm // G
    x = x_ref[...]
    h = _rms(x, gain_ref[...]).astype(BF16)
    ffn.advance(2)
    ffn_out = None
    for g in range(ngroups):
        rows = slice(g * G, (g + 1) * G)
        hp_s[rows, :] = jnp.dot(perm_ref[...], h[rows, :], preferred_element_type=F32).astype(BF16)
    hp = hp_s[...]
    sub = lax.broadcasted_iota(jnp.int32, (S8, 1), 0)
    neg = -lam_ref[...]
    softplus = jnp.maximum(neg, 0.0) + jnp.log1p(jnp.exp(-jnp.abs(neg)))
    coef = -LRU_C * softplus

    def project(p):
        return jnp.dot(hp, win_ref[:, 2 * p * LRU_PAIR_W:2 * (p + 1) * LRU_PAIR_W],
                       preferred_element_type=F32)

    proj_next = project(0)
    for p in range(LRU_PAIRS):
        cols = slice(p * LRU_PAIR_W, (p + 1) * LRU_PAIR_W)
        proj = proj_next
        if p + 1 < LRU_PAIRS:
            proj_next = project(p + 1)
        if p + 1 < LRU_PAIRS:
            ffn.advance((2, 3, 3)[p])
        else:
            ffn_out = ffn.finish()
        gate = proj[:, 0:LRU_PAIR_W]
        u = proj[:, LRU_PAIR_W:2 * LRU_PAIR_W]

        conv = []
        for g in range(ngroups):
            u_g = u[g * G:(g + 1) * G, :]
            tail_prev = tail_s[:, cols]
            tail_cur = u_g[G - LRU_TAIL:G, :]
            heads = []
            for k in range(CONV_W - 1):
                sl = slice(k * S8, (k + 1) * S8)
                heads.append(jnp.where(sub == 0, pltpu.roll(tail_prev[sl, :], 1, 0),
                                       pltpu.roll(tail_cur[sl, :], 1, 0)))
            tail_s[:, cols] = tail_cur
            u_ext = jnp.concatenate(heads + [u_g], axis=0)
            uc_g = cb_ref[:, cols] + cw_ref[0:1, cols] * u_ext[0:G, :]
            for j in range(1, CONV_W):
                uc_g = uc_g + cw_ref[j:j + 1, cols] * u_ext[j * S8:j * S8 + G, :]
            conv.append(uc_g)
        uc = jnp.concatenate(conv, axis=0)

        ri = jnp.dot(uc.astype(BF16), wri_ref[p], preferred_element_type=F32) + bri_ref[p]
        r = 0.5 * jnp.tanh(0.5 * ri[:, 0:LRU_PAIR_W]) + 0.5
        i = 0.5 * jnp.tanh(0.5 * ri[:, LRU_PAIR_W:2 * LRU_PAIR_W]) + 0.5
        log_a = coef[:, cols] * r
        a_s[:, cols] = jnp.exp(log_a)
        th = jnp.tanh(log_a)
        num = -2.0 * th
        root = jnp.where(num == 0.0, 0.0, num * lax.rsqrt(num * (1.0 - th)))
        b_s[:, cols] = root * (i * uc)

        h0 = hprev[:, cols]
        accs_a = [a_s[g * G:g * G + S8, cols] for g in range(ngroups)]
        accs_b = [b_s[g * G:g * G + S8, cols] for g in range(ngroups)]
        for i_row in range(1, NI):
            for g in range(ngroups):
                r_sl = slice(g * G + i_row * S8, g * G + (i_row + 1) * S8)
                a_i = a_s[r_sl, cols]
                accs_b[g] = a_i * accs_b[g] + b_s[r_sl, cols]
                accs_a[g] = a_i * accs_a[g]
                a_s[r_sl, cols] = accs_a[g]
                b_s[r_sl, cols] = accs_b[g]
        for g in range(ngroups):
            base = g * G
            acc_a, acc_b = accs_a[g], accs_b[g]
            d = 1
            while d < S8:
                keep = sub >= d
                prev_a = jnp.where(keep, pltpu.roll(acc_a, d, 0), 1.0)
                prev_b = jnp.where(keep, pltpu.roll(acc_b, d, 0), 0.0)
                acc_b = acc_a * prev_b + acc_b
                acc_a = acc_a * prev_a
                d *= 2
            h_end = acc_a * h0 + acc_b
            carry = jnp.where(sub == 0, h0, pltpu.roll(h_end, 1, 0))
            rows = slice(base, base + G)
            hs = (b_s[rows, cols].reshape(NI, S8, LRU_PAIR_W)
                  + a_s[rows, cols].reshape(NI, S8, LRU_PAIR_W) * carry[None]
                  ).reshape(G, LRU_PAIR_W)
            y = (hs * jax.nn.gelu(gate[rows, :])).astype(BF16)
            y_s[rows, cols] = jnp.dot(permt_ref[...], y, preferred_element_type=F32).astype(BF16)
            h0 = h_end[S8 - 1:S8, :]
        hprev[:, cols] = h0

    return ffn_out, x + jnp.dot(y_s[...], wout_ref[...], preferred_element_type=F32)


def _lru_perm():
    p = np.arange(LRU_GROUP)
    time_of_row = (p % SUBLANES) * LRU_NI + p // SUBLANES
    perm = (time_of_row[:, None] == np.arange(LRU_GROUP)[None, :]).astype(np.float32)
    return jnp.asarray(perm, BF16), jnp.asarray(perm.T, BF16)


def _lru_in_weights(w_in):
    parts = []
    for p in range(LRU_PAIRS):
        lo, hi = p * LRU_PAIR_W, (p + 1) * LRU_PAIR_W
        parts += [w_in[:, lo:hi], w_in[:, LRU_W + lo:LRU_W + hi]]
    return jnp.concatenate(parts, axis=1).astype(BF16)


def _lru_gate_weights(w_r, b_r, w_i, b_i):
    bw = LRU_BLOCK_W
    z = jnp.zeros((bw, bw), w_r.dtype)
    mats, biases = [], []
    for p in range(LRU_PAIRS):
        n0, n1 = 2 * p, 2 * p + 1
        top = jnp.concatenate([w_r[n0], z, w_i[n0], z], axis=1)
        bot = jnp.concatenate([z, w_r[n1], z, w_i[n1]], axis=1)
        mats.append(jnp.concatenate([top, bot], axis=0))
        biases.append(jnp.concatenate([b_r[n0], b_r[n1], b_i[n0], b_i[n1]])[None, :])
    return jnp.stack(mats).astype(BF16), jnp.stack(biases)


def _lagged_specs(n_tiles, tm):
    tile_in = pl.BlockSpec((tm, D_MODEL), lambda i: (jnp.minimum(i, n_tiles - 1), 0))
    tile_out = pl.BlockSpec((tm, D_MODEL), lambda i: (jnp.maximum(i - 1, 0), 0))
    return tile_in, tile_out


def _rglru_ffn(x2d, seq, gain, w_in, conv_w, conv_b, wri, bri, lam, w_out, fgain, wgu, wd, layer):
    tm = TM_LRU
    n_tiles = x2d.shape[0] // tm
    perm, permt = _lru_perm()
    tile_in, tile_out = _lagged_specs(n_tiles, tm)
    return pl.pallas_call(
        functools.partial(_lru_kernel, steps_per_seq=seq // tm, layer=layer),
        grid=(n_tiles + 1,),
        in_specs=[
            tile_in,
            _layer_spec(gain, layer),
            _const_spec(perm.shape),
            _const_spec(permt.shape),
            _const_spec(w_in.shape),
            _const_spec(conv_w.shape),
            _const_spec(conv_b.shape),
            _const_spec(wri.shape),
            _const_spec(bri.shape),
            _const_spec(lam.shape),
            _const_spec(w_out.shape),
            _layer_spec(fgain, layer),
            pl.BlockSpec(memory_space=pl.ANY),
            pl.BlockSpec(memory_space=pl.ANY),
        ],
        out_specs=tile_out,
        out_shape=jax.ShapeDtypeStruct(x2d.shape, F32),
        scratch_shapes=[
            pltpu.VMEM((tm, D_MODEL), BF16),
            pltpu.VMEM((tm, LRU_W), F32),
            pltpu.VMEM((tm, LRU_W), F32),
            pltpu.VMEM((tm, LRU_W), BF16),
            pltpu.VMEM((LRU_TAIL, LRU_W), F32),
            pltpu.VMEM((1, LRU_W), F32),
            pltpu.VMEM((tm, D_MODEL), F32),
            pltpu.VMEM((tm, FFN_H), BF16),
        ] + _resident_scratch(wgu) + _resident_scratch(wd),
        compiler_params=_params(1),
        name="rglru_ffn",
    )(x2d, gain, perm, permt, w_in, conv_w, conv_b, wri, bri, lam, w_out, fgain, wgu, wd)


def _swa_rope(t, cos, sin_signed, low_half):
    partner = jnp.where(low_half, pltpu.roll(t, LANES - SWA_DH // 2, 1),
                        pltpu.roll(t, SWA_DH // 2, 1))
    return t * cos + partner * sin_signed


def _swa_kernel(sinks_ref, x_ref, cos_ref, sin_ref, gain_ref, wqkv_ref, bqkv_ref, sel_ref,
                wout_ref, bout_ref, fgain_ref, wgu_hbm, wd_hbm, o_ref,
                q_s, kbuf, vbuf, att_s, bias_s, xmid_s, act_s,
                wgu_ref, wgu_stage, wgu_sem, wd_ref, wd_stage, wd_sem, *, steps_per_seq, layer):
    @pl.when(pl.program_id(0) == 0)
    def _():
        _load_bf16(wgu_hbm.at[layer], wgu_ref, wgu_stage, wgu_sem)
        _load_bf16(wd_hbm.at[layer], wd_ref, wd_stage, wd_sem)

    seq_start = pl.program_id(0) % steps_per_seq == 0

    @pl.when(seq_start)
    def _():
        halo = (SWA_BLOCK, SWA_KV_HEADS * LANES)
        kbuf[0:SWA_BLOCK, :] = jnp.zeros(halo, BF16)
        vbuf[0:SWA_BLOCK, :] = jnp.zeros(halo, BF16)

    mixer = functools.partial(
        _swa_mixer, seq_start=seq_start, sinks_ref=sinks_ref, x_ref=x_ref, cos_ref=cos_ref,
        sin_ref=sin_ref, gain_ref=gain_ref, wqkv_ref=wqkv_ref, bqkv_ref=bqkv_ref, sel_ref=sel_ref,
        wout_ref=wout_ref, bout_ref=bout_ref, q_s=q_s, kbuf=kbuf, vbuf=vbuf, att_s=att_s,
        bias_s=bias_s)
    make_ffn = functools.partial(_Ffn, xmid_s, fgain_ref, wgu_ref, wd_ref, act_s)
    _lagged_steps(mixer, make_ffn, o_ref, xmid_s)


def _swa_mixer(ffn, *, seq_start, sinks_ref, x_ref, cos_ref, sin_ref, gain_ref, wqkv_ref, bqkv_ref,
               sel_ref, wout_ref, bout_ref, q_s, kbuf, vbuf, att_s, bias_s):
    tm = x_ref.shape[0]
    L = SWA_BLOCK
    x = x_ref[...]
    h = _rms(x, gain_ref[...]).astype(BF16)
    grp = 4 * LANES

    def project(kh):
        return (jnp.dot(h, wqkv_ref[:, kh * grp:(kh + 1) * grp], preferred_element_type=F32)
                + bqkv_ref[:, kh * grp:(kh + 1) * grp])

    proj_next = project(0)
    ffn.advance(3)
    ffn_out = None

    qi = lax.broadcasted_iota(jnp.int32, (2 * L, 4 * L), 0) % L
    si = lax.broadcasted_iota(jnp.int32, (2 * L, 4 * L), 1) % (2 * L)
    dist = qi + L - si
    in_window = (dist >= 0) & (dist < SWA_WINDOW)
    has_prev = (si >= L) | jnp.logical_not(seq_start)
    bias_s[0] = jnp.where(in_window & has_prev, 0.0, SWA_NEG)
    bias_s[1] = jnp.where(in_window, 0.0, SWA_NEG)

    lane = lax.broadcasted_iota(jnp.int32, (1, LANES), 1)
    low_half = (lane % SWA_DH) < (SWA_DH // 2)
    cos = jnp.dot(cos_ref[...], sel_ref[0], precision=lax.Precision.HIGHEST,
                  preferred_element_type=F32)
    sin_signed = jnp.dot(sin_ref[...], sel_ref[1], precision=lax.Precision.HIGHEST,
                         preferred_element_type=F32)

    first_head = lane < SWA_DH
    row_is_top = lax.broadcasted_iota(jnp.int32, (2 * L, 1), 0) < L
    mixed = x + bout_ref[...]

    for kh in range(SWA_KV_HEADS):
        base = kh * 2 * LANES
        kcols = slice(kh * LANES, (kh + 1) * LANES)
        proj = proj_next
        if kh + 1 < SWA_KV_HEADS:
            proj_next = project(kh + 1)
        if kh + 1 < SWA_KV_HEADS:
            ffn.advance((2, 3, 3)[kh])
        else:
            ffn_out = ffn.finish()
        for j in range(2):
            q_s[:, base + j * LANES:base + (j + 1) * LANES] = (
                _swa_rope(proj[:, j * LANES:(j + 1) * LANES], cos, sin_signed, low_half)
                * (SWA_DH ** -0.5)).astype(BF16)
        kbuf[L:L + tm, kcols] = _swa_rope(
            proj[:, 2 * LANES:3 * LANES], cos, sin_signed, low_half).astype(BF16)
        vbuf[L:L + tm, kcols] = proj[:, 3 * LANES:4 * LANES].astype(BF16)

        blocks = range(tm // L)
        sinks = [jnp.where(row_is_top, sinks_ref[4 * kh + par], sinks_ref[4 * kh + 2 + par])
                 for par in range(2)]
        scores, vcats = [], []
        for c in blocks:
            rows = slice(c * L, (c + 1) * L)
            qst = jnp.concatenate(
                [q_s[rows, base:base + LANES], q_s[rows, base + LANES:base + 2 * LANES]], axis=0)
            kd = kbuf[c * L:c * L + 2 * L, kcols]
            vd = vbuf[c * L:c * L + 2 * L, kcols]
            zero = jnp.zeros_like(kd)
            kcat = jnp.concatenate([jnp.where(first_head, kd, zero),
                                    jnp.where(first_head, zero, kd)], axis=0)
            vcats.append(jnp.concatenate([jnp.where(first_head, vd, zero),
                                          jnp.where(first_head, zero, vd)], axis=0))
            sc = lax.dot_general(qst, kcat, (((1,), (1,)), ((), ())),
                                 preferred_element_type=F32) + bias_s[0 if c == 0 else 1]
            scores.append([sc[:, par * 2 * L:(par + 1) * 2 * L] for par in range(2)])
        maxes = [[jnp.maximum(jnp.max(scores[c][par], axis=-1, keepdims=True), sinks[par])
                  for par in range(2)] for c in blocks]
        expd = [[jnp.exp(scores[c][par] - maxes[c][par]) for par in range(2)] for c in blocks]
        dens = [[jnp.sum(expd[c][par], axis=-1, keepdims=True)
                 + jnp.exp(sinks[par] - maxes[c][par]) for par in range(2)] for c in blocks]
        for c in blocks:
            rows = slice(c * L, (c + 1) * L)
            probs = jnp.concatenate(
                [(expd[c][par] * (1.0 / dens[c][par])).astype(BF16) for par in range(2)], axis=1)
            out = jnp.dot(probs, vcats[c], preferred_element_type=F32)
            att_s[rows, base:base + LANES] = out[0:L].astype(BF16)
            att_s[rows, base + LANES:base + 2 * LANES] = out[L:2 * L].astype(BF16)
        mixed = mixed + jnp.dot(att_s[:, base:base + 2 * LANES], wout_ref[base:base + 2 * LANES, :],
                                preferred_element_type=F32)

    kbuf[0:L, :] = kbuf[tm:tm + L, :]
    vbuf[0:L, :] = vbuf[tm:tm + L, :]
    return ffn_out, mixed


def _swa_weights(w_qkv, b_qkv):
    nq = SWA_Q_HEADS * SWA_DH
    nk = SWA_KV_HEADS * SWA_DH
    group = SWA_Q_HEADS // SWA_KV_HEADS * SWA_DH

    def regroup(t):
        parts = []
        for kh in range(SWA_KV_HEADS):
            khead = t[..., nq + kh * SWA_DH:nq + (kh + 1) * SWA_DH]
            vhead = t[..., nq + nk + kh * SWA_DH:nq + nk + (kh + 1) * SWA_DH]
            parts += [t[..., kh * group:(kh + 1) * group], khead, khead, vhead, vhead]
        return jnp.concatenate(parts, axis=-1)

    return regroup(w_qkv).astype(BF16), regroup(b_qkv)[None, :]


def _swa_ffn(x2d, cos2d, sin2d, seq, gain, wqkv, bqkv, sinks, w_out, b_out, fgain, wgu, wd, layer):
    tm = TM_SWA
    n_tiles = x2d.shape[0] // tm
    lane = np.arange(LANES)
    picks = (np.arange(LANES)[:, None] == (RET_QK // SWA_DH) * (lane % (SWA_DH // 2))[None, :])
    sign = np.where(lane % SWA_DH < SWA_DH // 2, -1.0, 1.0)
    sel = jnp.asarray(np.stack([picks * 1.0, picks * sign[None, :]]), F32)
    table = pl.BlockSpec((tm, LANES), lambda i: (jnp.minimum(i, n_tiles - 1), 0))
    tile_in, tile_out = _lagged_specs(n_tiles, tm)
    return pl.pallas_call(
        functools.partial(_swa_kernel, steps_per_seq=seq // tm, layer=layer),
        grid=(n_tiles + 1,),
        in_specs=[
            pl.BlockSpec(memory_space=pltpu.SMEM),
            tile_in,
            table,
            table,
            _layer_spec(gain, layer),
            _const_spec(wqkv.shape),
            _const_spec(bqkv.shape),
            _const_spec(sel.shape),
            _const_spec(w_out.shape),
            _const_spec((1, D_MODEL)),
            _layer_spec(fgain, layer),
            pl.BlockSpec(memory_space=pl.ANY),
            pl.BlockSpec(memory_space=pl.ANY),
        ],
        out_specs=tile_out,
        out_shape=jax.ShapeDtypeStruct(x2d.shape, F32),
        scratch_shapes=[
            pltpu.VMEM((tm, SWA_Q_HEADS * SWA_DH), BF16),
            pltpu.VMEM((SWA_BLOCK + tm, SWA_KV_HEADS * LANES), BF16),
            pltpu.VMEM((SWA_BLOCK + tm, SWA_KV_HEADS * LANES), BF16),
            pltpu.VMEM((tm, SWA_Q_HEADS * SWA_DH), BF16),
            pltpu.VMEM((2, 2 * SWA_BLOCK, 4 * SWA_BLOCK), F32),
            pltpu.VMEM((tm, D_MODEL), F32),
            pltpu.VMEM((tm, FFN_H), BF16),
        ] + _resident_scratch(wgu) + _resident_scratch(wd),
        compiler_params=_params(1),
        name="swa_ffn",
    )(sinks, x2d, cos2d, sin2d, gain, wqkv, bqkv, sel, w_out, b_out, fgain, wgu, wd)


def kernel(x, positions, mix_norm, ffn_norm, final_norm, ret_w_in, ret_w_out, lru_w_in, lru_conv_w, lru_conv_b, lru_w_r, lru_b_r, lru_w_i, lru_b_i, lru_lambda, lru_w_out, swa_w_qkv, swa_b_qkv, swa_sinks, swa_w_out, swa_b_out, ffn_w_gu, ffn_w_down):
    b, s, d = x.shape
    pos3 = positions.reshape(b, s, 1)
    fin = final_norm[None, :]
    mix_gain = mix_norm[:, None, :]
    ffn_gain = ffn_norm[:, None, :]
    ret_in, ret_out, ffn_gu, ffn_down = ret_w_in, ret_w_out, ffn_w_gu, ffn_w_down
    rope = None
    for layer in range(DEPTH):
        kind = layer % N_MIXERS
        j = layer // N_MIXERS
        if kind == 0:
            if rope is None:
                x, cos, sin = _retention(x, pos3, mix_gain, ret_in, ret_out, layer, j)
                rope = (cos, sin)
            else:
                x = _retention(x, rope, mix_gain, ret_in, ret_out, layer, j)
            x = _ffn(x.reshape(b * s, d), ffn_gain, ffn_gu, ffn_down, fin, layer,
                     layer == DEPTH - 1).reshape(b, s, d)
        elif kind == 1:
            wri, bri = _lru_gate_weights(lru_w_r[j], lru_b_r[j], lru_w_i[j], lru_b_i[j])
            x = _rglru_ffn(x.reshape(b * s, d), s, mix_gain, _lru_in_weights(lru_w_in[j]),
                           lru_conv_w[j], lru_conv_b[j][None, :], wri, bri, lru_lambda[j][None, :],
                           lru_w_out[j].astype(BF16), ffn_gain, ffn_gu, ffn_down,
                           layer).reshape(b, s, d)
        else:
            wqkv, bqkv = _swa_weights(swa_w_qkv[j], swa_b_qkv[j])
            x = _swa_ffn(x.reshape(b * s, d), rope[0].reshape(b * s, LANES),
                         rope[1].reshape(b * s, LANES), s, mix_gain, wqkv, bqkv, swa_sinks[j],
                         swa_w_out[j].astype(BF16), swa_b_out[j][None, :], ffn_gain, ffn_gu,
                         ffn_down, layer).reshape(b, s, d)
    return x
```
